```python
import jax
import jax.numpy as jnp
from jax import lax
import numpy as np

D_MODEL = 1024
BATCH = 8
SEQ = 4096
DEPTH = 2
DEC_BATCH = 32
DEC_SEQ = 1
PAST_LEN = 16384
PAGE_SIZE = 128

HEAD_DIM = 64
N_HEADS = D_MODEL // HEAD_DIM
H_SB = N_HEADS // 4
H_FOX = (N_HEADS - H_SB) // 2
H_MOBA = N_HEADS - H_SB - H_FOX
PROJ_COLS = 3 * N_HEADS * HEAD_DIM + H_FOX
ROPE_DIM = HEAD_DIM // 4
ROPE_THETA = 500000.0
Q_BLOCK = 128
MOBA_BLOCK = 256
MOBA_TOPK = 3
MOBA_Q_BLOCK = 32
D_FF = 2816
N_EXPERTS = 8
TOP_K = 2
D_FF_EXPERT = 2816
MOE_BLOCK = 128
N_DENSE = (DEPTH + 1) // 2
N_MOE = DEPTH // 2
DEEPNORM_ALPHA = (2 * DEPTH) ** 0.25
DEEPNORM_BETA = (8 * DEPTH) ** -0.25
ATTN_SCALE = HEAD_DIM ** -0.5
LN_EPS = 1e-5
RMS_EPS = 1e-6
FORGET_BIAS_LO = 2.0
FORGET_BIAS_HI = 6.0
CACHE_FORGET_BIAS = 4.0

kernel_name = 'hybrid_sb_fox_moba_deepnorm_step'


def layer_norm(x, g, b):
    xf = x.astype(jnp.float32)
    mu = jnp.mean(xf, axis=-1, keepdims=True)
    var = jnp.mean(jnp.square(xf - mu), axis=-1, keepdims=True)
    return ((xf - mu) * lax.rsqrt(var + LN_EPS) * g + b).astype(x.dtype)


def merge_groups(groups, g, w_out):
    normed = []
    for o in groups:
        of = o.reshape(o.shape[0], o.shape[1], -1).astype(jnp.float32)
        normed.append(of * lax.rsqrt(jnp.mean(jnp.square(of), axis=-1, keepdims=True) + RMS_EPS))
    h = (jnp.concatenate(normed, axis=-1) * g).astype(groups[0].dtype)
    return jnp.einsum('bte,ed->btd', h, w_out)


def rope(x, pos):
    half = ROPE_DIM // 2
    inv = ROPE_THETA ** (-jnp.arange(half, dtype=jnp.float32) / half)
    ang = pos.astype(jnp.float32)[:, None] * inv
    cos = jnp.cos(ang)[None, :, None, :]
    sin = jnp.sin(ang)[None, :, None, :]
    x1 = x[..., :half].astype(jnp.float32)
    x2 = x[..., half:ROPE_DIM].astype(jnp.float32)
    rot = jnp.concatenate([x1 * cos - x2 * sin, x2 * cos + x1 * sin], axis=-1).astype(x.dtype)
    return jnp.concatenate([rot, x[..., ROPE_DIM:]], axis=-1)


def project(x, w_in, b_f):
    b, t, _ = x.shape
    hd = N_HEADS * HEAD_DIM
    p = jnp.einsum('btd,dc->btc', x, w_in)
    q = p[..., :hd].reshape(b, t, N_HEADS, HEAD_DIM)
    k = p[..., hd:2 * hd].reshape(b, t, N_HEADS, HEAD_DIM)
    v = p[..., 2 * hd:3 * hd].reshape(b, t, N_HEADS, HEAD_DIM)
    logf = jax.nn.log_sigmoid(p[..., 3 * hd:].astype(jnp.float32) + b_f.astype(jnp.float32))
    return q, k, v, logf


def split_heads(t):
    return t[:, :, :H_SB], t[:, :, H_SB:H_SB + H_FOX], t[:, :, H_SB + H_FOX:]


def sweep(fn, length, blk):
    out = lax.map(fn, jnp.arange(0, length, blk))
    n, b = out.shape[0], out.shape[1]
    return jnp.moveaxis(out, 0, 1).reshape(b, n * blk, *out.shape[3:])


def sb_attend(q, k, v, q_pos, k_pos):
    z = jnp.einsum('bqhd,bkhd->bhqk', q, k).astype(jnp.float32) * ATTN_SCALE
    mask = k_pos[None, :] < q_pos[:, None]
    log_1m = jnp.where(mask, jax.nn.log_sigmoid(-z), 0.0)
    suffix = lax.cumsum(log_1m, axis=3, reverse=True) - log_1m
    w = jnp.where(mask, jnp.exp(jax.nn.log_sigmoid(z) + suffix), 0.0)
    return jnp.einsum('bhqk,bkhd->bqhd', w.astype(v.dtype), v)


def fox_attend(q, k, v, c_q, c_k, q_pos, k_pos):
    s = jnp.einsum('bqhd,bkhd->bhqk', q, k).astype(jnp.float32) * ATTN_SCALE
    s = s + jnp.transpose(c_q, (0, 2, 1))[:, :, :, None] - jnp.transpose(c_k, (0, 2, 1))[:, :, None, :]
    mask = k_pos[None, :] <= q_pos[:, None]
    p = jax.nn.softmax(jnp.where(mask, s, -jnp.inf), axis=-1)
    return jnp.einsum('bhqk,bkhd->bqhd', p.astype(v.dtype), v)


def moba_blocks(k, v):
    b, l, h, d = k.shape
    nb = -(-l // MOBA_BLOCK)
    pad = nb * MOBA_BLOCK - l
    k = jnp.pad(k, ((0, 0), (0, pad), (0, 0), (0, 0)))
    v = jnp.pad(v, ((0, 0), (0, pad), (0, 0), (0, 0)))
    k_bh = jnp.transpose(k.reshape(b, nb, MOBA_BLOCK, h, d), (0, 3, 1, 2, 4))
    v_bh = jnp.transpose(v.reshape(b, nb, MOBA_BLOCK, h, d), (0, 3, 1, 2, 4))
    k_mean = jnp.mean(k_bh.astype(jnp.float32), axis=3).astype(k.dtype)
    return k_bh, v_bh, k_mean


def _gather_blocks(blocks, idx):
    return blocks[idx]


def moba_attend(q, q_pos, k_bh, v_bh, k_mean):
    b, tq, h, _ = q.shape
    nb = k_bh.shape[2]
    own = q_pos // MOBA_BLOCK
    gate = jnp.einsum('bqhd,bhnd->bhqn', q, k_mean).astype(jnp.float32)
    past = jnp.arange(nb)[None, :] < own[:, None]
    gate = jnp.where(past, gate, -jnp.inf)
    _, top_idx = lax.top_k(gate, min(MOBA_TOPK, nb))
    sel_ok = top_idx < own[:, None]
    own_idx = jnp.broadcast_to(own[:, None].astype(top_idx.dtype), (b, h, tq, 1))
    idx = jnp.concatenate([top_idx, own_idx], axis=-1)
    ok = jnp.concatenate([sel_ok, jnp.ones((b, h, tq, 1), dtype=bool)], axis=-1)
    gather = jax.vmap(jax.vmap(_gather_blocks))
    gk = gather(k_bh, idx)
    gv = gather(v_bh, idx)
    kpos = idx[..., None] * MOBA_BLOCK + jnp.arange(MOBA_BLOCK)
    mask = ok[..., None] & (kpos <= q_pos[:, None, None])
    s = jnp.einsum('bqhd,bhqsnd->bhqsn', q, gk).astype(jnp.float32) * ATTN_SCALE
    s = jnp.where(mask, s, -jnp.inf)
    p = jax.nn.softmax(s.reshape(b, h, tq, -1), axis=-1).reshape(s.shape)
    return jnp.einsum('bhqsn,bhqsnd->bqhd', p.astype(gv.dtype), gv)


def mix_prompt(x, w_in, b_f, norm_g, w_out):
    b, t, _ = x.shape
    pos = jnp.arange(t)
    q, k, v, logf = project(x, w_in, b_f)
    qa, qb, qc = split_heads(q)
    ka, kb, kc = split_heads(k)
    va, vb, vc = split_heads(v)
    qc = rope(qc, pos)
    kc = rope(kc, pos)
    c = jnp.cumsum(logf, axis=1)

    def sb_block(i):
        qs = lax.dynamic_slice_in_dim(qa, i, Q_BLOCK, axis=1)
        return sb_attend(qs, ka, va, i + jnp.arange(Q_BLOCK), pos)

    def fox_block(i):
        qs = lax.dynamic_slice_in_dim(qb, i, Q_BLOCK, axis=1)
        cq = lax.dynamic_slice_in_dim(c, i, Q_BLOCK, axis=1)
        return fox_attend(qs, kb, vb, cq, c, i + jnp.arange(Q_BLOCK), pos)

    k_bh, v_bh, k_mean = moba_blocks(kc, vc)

    def moba_block(i):
        qs = lax.dynamic_slice_in_dim(qc, i, MOBA_Q_BLOCK, axis=1)
        return moba_attend(qs, i + jnp.arange(MOBA_Q_BLOCK), k_bh, v_bh, k_mean)

    oa = sweep(sb_block, t, Q_BLOCK)
    ob = sweep(fox_block, t, Q_BLOCK)
    oc = sweep(moba_block, t, MOBA_Q_BLOCK)
    y = merge_groups([oa, ob, oc], norm_g, w_out)
    k_rows = jnp.concatenate([ka, kb, kc], axis=2)
    return y, k_rows, v, logf.astype(x.dtype)


def mix_sample(x, ck, cv, cl, page_table, w_in, b_f, norm_g, w_out):
    b, t, _ = x.shape
    past = page_table.shape[1] * ck.shape[1]
    q_pos = past + jnp.arange(t)
    k_pos = jnp.arange(past + t)
    q, k, v, logf = project(x, w_in, b_f)
    qa, qb, qc = split_heads(q)
    ka, kb, kc = split_heads(k)
    qc = rope(qc, q_pos)
    kc = rope(kc, q_pos)
    k_new = jnp.concatenate([ka, kb, kc], axis=2)
    past_k = ck[page_table].reshape(b, past, N_HEADS, HEAD_DIM)
    past_v = cv[page_table].reshape(b, past, N_HEADS, HEAD_DIM)
    past_l = cl[page_table].reshape(b, past, H_FOX).astype(jnp.float32)
    k_all = jnp.concatenate([past_k, k_new], axis=1)
    v_all = jnp.concatenate([past_v, v], axis=1)
    ka_all, kb_all, kc_all = split_heads(k_all)
    va_all, vb_all, vc_all = split_heads(v_all)
    oa = sb_attend(qa, ka_all, va_all, q_pos, k_pos)
    c_past = past_l - lax.cumsum(past_l, axis=1, reverse=True)
    c_new = jnp.cumsum(logf, axis=1)
    ob = fox_attend(qb, kb_all, vb_all, c_new, jnp.concatenate([c_past, c_new], axis=1), q_pos, k_pos)
    k_bh, v_bh, k_mean = moba_blocks(kc_all, vc_all)
    oc = moba_attend(qc, q_pos, k_bh, v_bh, k_mean)
    y = merge_groups([oa, ob, oc], norm_g, w_out)
    return y, k_new, v, logf.astype(x.dtype)


def swiglu(x, w1, w3, w2):
    return jnp.matmul(jax.nn.silu(jnp.matmul(x, w1)) * jnp.matmul(x, w3), w2)


def moe_swiglu(x2d, w_r, b_r, w1, w3, w2):
    n, d = x2d.shape
    logits = (jnp.matmul(x2d, w_r) + b_r).astype(jnp.float32)
    top_v, top_i = lax.top_k(logits, TOP_K)
    gates = jax.nn.softmax(top_v, axis=-1).astype(x2d.dtype)
    a = n * TOP_K
    e = top_i.reshape(a)
    tok = jnp.repeat(jnp.arange(n, dtype=jnp.int32), TOP_K)
    g = gates.reshape(a)
    order = jnp.argsort(e)
    e_s, tok_s, g_s = e[order], tok[order], g[order]
    counts = jnp.bincount(e, length=N_EXPERTS)
    padded = (counts + MOE_BLOCK - 1) // MOE_BLOCK * MOE_BLOCK
    pad_end = jnp.cumsum(padded)
    pad_start = pad_end - padded
    raw_start = jnp.cumsum(counts) - counts
    dest = pad_start[e_s] + jnp.arange(a) - raw_start[e_s]
    n_blocks = (a + N_EXPERTS * (MOE_BLOCK - 1) + MOE_BLOCK - 1) // MOE_BLOCK
    rows = n_blocks * MOE_BLOCK
    x_buf = jnp.zeros((rows, d), x2d.dtype).at[dest].set(x2d[tok_s])
    tok_buf = jnp.zeros((rows,), jnp.int32).at[dest].set(tok_s)
    g_buf = jnp.zeros((rows,), x2d.dtype).at[dest].set(g_s)
    blk_expert = jnp.minimum(jnp.searchsorted(pad_end, jnp.arange(n_blocks) * MOE_BLOCK, side='right'), N_EXPERTS - 1)

    def expert_block(args):
        xb, ei = args
        return swiglu(xb, w1[ei], w3[ei], w2[ei])

    y_buf = lax.map(expert_block, (x_buf.reshape(n_blocks, MOE_BLOCK, d), blk_expert)).reshape(rows, d)
    return jnp.zeros_like(x2d).at[tok_buf].add(y_buf * g_buf[:, None])


def setup_inputs(seed: int = 0) -> dict:
    key = jax.random.key(seed)
    ks = jax.random.split(key, 24)
    f32 = jnp.float32
    nrm = jax.random.normal
    n_pages = PAST_LEN // PAGE_SIZE
    n_used = DEC_BATCH * n_pages
    n_pool = n_used + (n_used + 3) // 4
    hd = N_HEADS * HEAD_DIM
    std = D_MODEL ** -0.5
    x_prompt = nrm(ks[0], (BATCH, SEQ, D_MODEL), f32)
    x_sample = nrm(ks[1], (DEC_BATCH, DEC_SEQ, D_MODEL), f32)
    cache_k = nrm(ks[2], (DEPTH, n_pool, PAGE_SIZE, N_HEADS, HEAD_DIM), f32)
    cache_v = nrm(ks[3], (DEPTH, n_pool, PAGE_SIZE, N_HEADS, HEAD_DIM), f32) * DEEPNORM_BETA
    cache_logf = jax.nn.log_sigmoid(CACHE_FORGET_BIAS + nrm(ks[4], (DEPTH, n_pool, PAGE_SIZE, H_FOX), f32))
    page_table = jax.random.permutation(ks[5], n_pool)[:n_used].reshape(DEC_BATCH, n_pages).astype(jnp.int32)
    col_scale = jnp.concatenate([jnp.ones((2 * hd,), f32), jnp.full((hd,), DEEPNORM_BETA, f32), jnp.ones((H_FOX,), f32)])
    w_in = nrm(ks[6], (DEPTH, D_MODEL, PROJ_COLS), f32) * std * col_scale
    b_f = jax.random.uniform(ks[7], (DEPTH, H_FOX), f32, minval=FORGET_BIAS_LO, maxval=FORGET_BIAS_HI)
    mix_norm_g = 1.0 + 0.02 * nrm(ks[8], (DEPTH, D_MODEL), f32)
    w_out = nrm(ks[9], (DEPTH, D_MODEL, D_MODEL), f32) * std * DEEPNORM_BETA
    ln1_g = 1.0 + 0.02 * nrm(ks[10], (DEPTH, D_MODEL), f32)
    ln1_b = 0.02 * nrm(ks[11], (DEPTH, D_MODEL), f32)
    ln2_g = 1.0 + 0.02 * nrm(ks[12], (DEPTH, D_MODEL), f32)
    ln2_b = 0.02 * nrm(ks[13], (DEPTH, D_MODEL), f32)
    ffn_w1 = nrm(ks[14], (N_DENSE, D_MODEL, D_FF), f32) * std * DEEPNORM_BETA
    ffn_w3 = nrm(ks[15], (N_DENSE, D_MODEL, D_FF), f32) * std * DEEPNORM_BETA
    ffn_w2 = nrm(ks[16], (N_DENSE, D_FF, D_MODEL), f32) * (D_FF ** -0.5) * DEEPNORM_BETA
    router_w = nrm(ks[17], (N_MOE, D_MODEL, N_EXPERTS), f32) * std
    router_b = 0.01 * nrm(ks[18], (N_MOE, N_EXPERTS), f32)
    moe_w1 = nrm(ks[19], (N_MOE, N_EXPERTS, D_MODEL, D_FF_EXPERT), f32) * std * DEEPNORM_BETA
    moe_w3 = nrm(ks[20], (N_MOE, N_EXPERTS, D_MODEL, D_FF_EXPERT), f32) * std * DEEPNORM_BETA
    moe_w2 = nrm(ks[21], (N_MOE, N_EXPERTS, D_FF_EXPERT, D_MODEL), f32) * (D_FF_EXPERT ** -0.5) * DEEPNORM_BETA
    return {'x_prompt': x_prompt, 'x_sample': x_sample, 'cache_k': cache_k, 'cache_v': cache_v,
            'cache_logf': cache_logf, 'page_table': page_table, 'w_in': w_in, 'b_f': b_f,
            'mix_norm_g': mix_norm_g, 'w_out': w_out, 'ln1_g': ln1_g, 'ln1_b': ln1_b,
            'ln2_g': ln2_g, 'ln2_b': ln2_b, 'ffn_w1': ffn_w1, 'ffn_w3': ffn_w3, 'ffn_w2': ffn_w2,
            'router_w': router_w, 'router_b': router_b, 'moe_w1': moe_w1, 'moe_w3': moe_w3, 'moe_w2': moe_w2}


def reference(x_prompt, x_sample, cache_k, cache_v, cache_logf, page_table, w_in, b_f, mix_norm_g, w_out,
              ln1_g, ln1_b, ln2_g, ln2_b, ffn_w1, ffn_w3, ffn_w2, router_w, router_b, moe_w1, moe_w3, moe_w2):
    hp, hs = x_prompt, x_sample
    kp, vp, lp, ksm, vsm, lsm = [], [], [], [], [], []
    for l in range(DEPTH):
        mp, k1, v1, f1 = mix_prompt(hp, w_in[l], b_f[l], mix_norm_g[l], w_out[l])
        ms, k2, v2, f2 = mix_sample(hs, cache_k[l], cache_v[l], cache_logf[l], page_table,
                                    w_in[l], b_f[l], mix_norm_g[l], w_out[l])
        hp = layer_norm(DEEPNORM_ALPHA * hp + mp, ln1_g[l], ln1_b[l])
        hs = layer_norm(DEEPNORM_ALPHA * hs + ms, ln1_g[l], ln1_b[l])
        j = l // 2
        if l % 2 == 0:
            fp = swiglu(hp, ffn_w1[j], ffn_w3[j], ffn_w2[j])
            fs = swiglu(hs, ffn_w1[j], ffn_w3[j], ffn_w2[j])
        else:
            fp = moe_swiglu(hp.reshape(-1, D_MODEL), router_w[j], router_b[j], moe_w1[j], moe_w3[j], moe_w2[j]).reshape(hp.shape)
            fs = moe_swiglu(hs.reshape(-1, D_MODEL), router_w[j], router_b[j], moe_w1[j], moe_w3[j], moe_w2[j]).reshape(hs.shape)
        hp = layer_norm(DEEPNORM_ALPHA * hp + fp, ln2_g[l], ln2_b[l])
        hs = layer_norm(DEEPNORM_ALPHA * hs + fs, ln2_g[l], ln2_b[l])
        kp.append(k1)
        vp.append(v1)
        lp.append(f1)
        ksm.append(k2)
        vsm.append(v2)
        lsm.append(f2)
    return (hp, hs, jnp.stack(kp), jnp.stack(vp), jnp.stack(lp), jnp.stack(ksm), jnp.stack(vsm), jnp.stack(lsm))
```

```python
import functools

import jax
import jax.numpy as jnp
from jax import lax
from jax.experimental import pallas as pl
from jax.experimental.pallas import tpu as pltpu

F32 = jnp.float32
BF16 = jnp.bfloat16

HEAD_DIM = 64
N_HEADS = 16
H_SB = 4
H_FOX = 6
H_MOBA = 6
D_MODEL = N_HEADS * HEAD_DIM
ROPE_DIM = HEAD_DIM // 4
ROPE_THETA = 500000.0
MOBA_BLOCK = 256
MOBA_TOPK = 3
N_EXPERTS = 8
TOP_K = 2
DEPTH = 2
DEEPNORM_ALPHA = (2 * DEPTH) ** 0.25
ATTN_SCALE = HEAD_DIM ** -0.5
LN_EPS = 1e-5
RMS_EPS = 1e-6

NEG = -1e30
VMEM_LIMIT = 48 * 1024 * 1024
ATT_BLOCK = 256
LANES = 128


def _cparams(*sem):
    return pltpu.CompilerParams(dimension_semantics=sem, vmem_limit_bytes=VMEM_LIMIT)


def _dot(a, b):
    return jnp.dot(a, b, preferred_element_type=F32)


def _dot_nt(a, b):
    return lax.dot_general(a, b, (((1,), (1,)), ((), ())), preferred_element_type=F32)


def _split3(x):
    hi = x.astype(BF16)
    r = x - hi.astype(F32)
    mid = r.astype(BF16)
    lo = (r - mid.astype(F32)).astype(BF16)
    return hi, mid, lo


def _softplus(z):
    return jnp.maximum(z, 0.0) + jnp.log1p(jnp.exp(-jnp.abs(z)))


def _mm_kernel(x_ref, w_ref, o_ref, *, precise):
    x = x_ref[...]
    w = w_ref[...]
    if precise:
        xh = x.astype(BF16)
        xl = (x - xh.astype(F32)).astype(BF16)
        wh = w.astype(BF16)
        wl = (w - wh.astype(F32)).astype(BF16)
        o_ref[...] = _dot(xh, wh) + (_dot(xh, wl) + _dot(xl, wh))
    else:
        o_ref[...] = _dot(x.astype(BF16), w.astype(BF16))


def _mm(x, w, tm, tn, precise=False):
    m, k = x.shape
    n = w.shape[1]
    assert m % tm == 0 and n % tn == 0, (x.shape, w.shape, tm, tn)
    return pl.pallas_call(
        functools.partial(_mm_kernel, precise=precise),
        grid=(m // tm, n // tn),
        in_specs=[pl.BlockSpec((tm, k), lambda i, j: (i, 0)),
                  pl.BlockSpec((k, tn), lambda i, j: (0, j))],
        out_specs=pl.BlockSpec((tm, tn), lambda i, j: (i, j)),
        out_shape=jax.ShapeDtypeStruct((m, n), F32),
        compiler_params=_cparams("parallel", "arbitrary"),
        name="matmul",
    )(x, w)


def _ffn_kernel(be_ref, x_ref, g_ref, w1_ref, w3_ref, w2_ref, o_ref, *, tf, gated):
    del be_ref
    x = x_ref[...].astype(BF16)
    d_ff = w1_ref.shape[2]
    for c in range(d_ff // tf):
        sl = slice(c * tf, (c + 1) * tf)
        h1 = _dot(x, w1_ref[0, :, sl])
        h3 = _dot(x, w3_ref[0, :, sl])
        h = (h1 * jax.nn.sigmoid(h1)) * h3
        y = _dot(h.astype(BF16), w2_ref[0, sl, :])
        if c == 0:
            o_ref[...] = y
        else:
            o_ref[...] += y
    if gated:
        o_ref[...] = o_ref[...] * g_ref[...]


def _ffn(x, blk_expert, gate, w1, w3, w2, tm, gated):
    rows, d = x.shape
    d_ff = w1.shape[2]
    tf = 256
    assert rows % tm == 0 and d_ff % tf == 0
    resident = pl.Buffered(1)
    grid_spec = pltpu.PrefetchScalarGridSpec(
        num_scalar_prefetch=1,
        grid=(rows // tm,),
        in_specs=[pl.BlockSpec((tm, d), lambda i, be: (i, 0)),
                  pl.BlockSpec((tm, 1), lambda i, be: (i, 0)),
                  pl.BlockSpec((1, d, d_ff), lambda i, be: (be[i], 0, 0), pipeline_mode=resident),
                  pl.BlockSpec((1, d, d_ff), lambda i, be: (be[i], 0, 0), pipeline_mode=resident),
                  pl.BlockSpec((1, d_ff, d), lambda i, be: (be[i], 0, 0), pipeline_mode=resident)],
        out_specs=pl.BlockSpec((tm, d), lambda i, be: (i, 0)),
    )
    return pl.pallas_call(
        functools.partial(_ffn_kernel, tf=tf, gated=gated),
        grid_spec=grid_spec,
        out_shape=jax.ShapeDtypeStruct((rows, d), F32),
        compiler_params=_cparams("arbitrary"),
        name="swiglu",
    )(blk_expert, x, gate, w1, w3, w2)


def _tile_iotas(i, j, tq, tk):
    rows = i * tq + lax.broadcasted_iota(jnp.int32, (tq, 1), 0)
    cols = j * tk + lax.broadcasted_iota(jnp.int32, (1, tk), 1)
    return rows, cols


def _kv_tile(ref, j, tk):
    return ref[0, 0, pl.ds(pl.multiple_of(j * tk, tk), tk), :]


def _sb_kernel(q_ref, k_ref, v_ref, o_ref, *, tq, tk):
    i = pl.program_id(2)
    q = q_ref[0, 0]
    upper = (lax.broadcasted_iota(jnp.int32, (tk, tk), 0) >
             lax.broadcasted_iota(jnp.int32, (tk, tk), 1)).astype(BF16)

    def body(jj, carry):
        acc, run = carry
        j = i - jj
        z = _dot_nt(q, _kv_tile(k_ref, j, tk)) * ATTN_SCALE
        sp = _softplus(z)
        rows, cols = _tile_iotas(i, j, tq, tk)
        mask = cols < rows
        l1m = jnp.where(mask, -sp, 0.0)
        hi = l1m.astype(BF16)
        lo = (l1m - hi.astype(F32)).astype(BF16)
        suffix = _dot(hi, upper) + _dot(lo, upper)
        w = jnp.where(mask, jnp.exp(z - sp + suffix + run), 0.0)
        acc = acc + _dot(w.astype(BF16), _kv_tile(v_ref, j, tk))
        run = run + jnp.sum(l1m, axis=1, keepdims=True)
        return acc, run

    acc, _ = lax.fori_loop(0, i + 1, body, (jnp.zeros((tq, HEAD_DIM), F32), jnp.zeros((tq, 1), F32)))
    o_ref[0, 0] = acc


def _fox_kernel(q_ref, k_ref, v_ref, c_ref, o_ref, *, tq, tk):
    i = pl.program_id(2)
    q = q_ref[0, 0]

    def body(j, carry):
        acc, m, l = carry
        s = _dot_nt(q, _kv_tile(k_ref, j, tk)) * ATTN_SCALE - c_ref[0, 0, pl.ds(j, 1), :]
        rows, cols = _tile_iotas(i, j, tq, tk)
        s = jnp.where(cols <= rows, s, NEG)
        m_new = jnp.maximum(m, jnp.max(s, axis=1, keepdims=True))
        alpha = jnp.exp(m - m_new)
        p = jnp.exp(s - m_new)
        l = alpha * l + jnp.sum(p, axis=1, keepdims=True)
        acc = alpha * acc + _dot(p.astype(BF16), _kv_tile(v_ref, j, tk))
        return acc, m_new, l

    init = (jnp.zeros((tq, HEAD_DIM), F32), jnp.full((tq, 1), NEG, F32), jnp.zeros((tq, 1), F32))
    acc, _, l = lax.fori_loop(0, i + 1, body, init)
    o_ref[0, 0] = acc / l


def _moba_kernel(q_ref, k_ref, v_ref, km_ref, o_ref, *, tq, tk):
    i = pl.program_id(2)
    q = q_ref[0, 0]
    lane = lax.broadcasted_iota(jnp.int32, (tq, LANES), 1)
    gate = jnp.where(lane < i, _dot_nt(q, km_ref[0, 0]), -jnp.inf)
    sel = jnp.zeros((tq, LANES), F32)
    for _ in range(MOBA_TOPK):
        best = jnp.max(gate, axis=1, keepdims=True)
        first = jnp.min(jnp.where(gate == best, lane, LANES), axis=1, keepdims=True)
        pick = lane == first
        sel = jnp.where(pick & (best > -jnp.inf), 1.0, sel)
        gate = jnp.where(pick, -jnp.inf, gate)

    def step(j, carry, mask):
        acc, m, l = carry
        s = jnp.where(mask, _dot_nt(q, _kv_tile(k_ref, j, tk)) * ATTN_SCALE, NEG)
        m_new = jnp.maximum(m, jnp.max(s, axis=1, keepdims=True))
        alpha = jnp.exp(m - m_new)
        p = jnp.where(mask, jnp.exp(s - m_new), 0.0)
        l = alpha * l + jnp.sum(p, axis=1, keepdims=True)
        acc = alpha * acc + _dot(p.astype(BF16), _kv_tile(v_ref, j, tk))
        return acc, m_new, l

    def body(j, carry):
        chosen = jnp.sum(jnp.where(lane == j, sel, 0.0), axis=1, keepdims=True) > 0.0
        return step(j, carry, chosen)

    init = (jnp.zeros((tq, HEAD_DIM), F32), jnp.full((tq, 1), NEG, F32), jnp.zeros((tq, 1), F32))
    carry = lax.fori_loop(0, i, body, init)
    rows, cols = _tile_iotas(i, i, tq, tk)
    acc, _, l = step(i, carry, cols <= rows)
    o_ref[0, 0] = acc / l


def _prompt_attention(kind, q, k, v, extra=None):
    b, h, t, d = q.shape
    tq = tk = ATT_BLOCK
    assert t % tq == 0 and d == HEAD_DIM
    q_spec = pl.BlockSpec((1, 1, tq, d), lambda bi, hi, i: (bi, hi, i, 0))
    kv_spec = pl.BlockSpec((1, 1, t, d), lambda bi, hi, i: (bi, hi, 0, 0))
    in_specs = [q_spec, kv_spec, kv_spec]
    args = [q, k, v]
    if kind == "sb":
        body = _sb_kernel
    elif kind == "fox":
        body = _fox_kernel
        in_specs.append(pl.BlockSpec((1, 1, t // tk, tk), lambda bi, hi, i: (bi, hi, 0, 0)))
        args.append(extra)
    else:
        body = _moba_kernel
        in_specs.append(pl.BlockSpec((1, 1, LANES, d), lambda bi, hi, i: (bi, hi, 0, 0)))
        args.append(extra)
    return pl.pallas_call(
        functools.partial(body, tq=tq, tk=tk),
        grid=(b, h, t // tq),
        in_specs=in_specs,
        out_specs=q_spec,
        out_shape=jax.ShapeDtypeStruct((b, h, t, d), F32),
        compiler_params=_cparams("parallel", "parallel", "arbitrary"),
        name=kind + "_prompt",
    )(*args)


def _decode_kernel(pt_ref, q_ref, kn_ref, vn_ref, lfn_ref, ka_ref, kb_ref, va_ref, vb_ref, lfa_ref, lfb_ref,
                   o_ref, acc_ref, m_ref, l_ref, sbrun_ref, lrun_ref, sg_ref, sm_ref, sl_ref, sacc_ref,
                   *, page):
    del pt_ref
    n = pl.program_id(1)
    nh, d = N_HEADS, HEAD_DIM
    blk = 2 * page
    row = lax.broadcasted_iota(jnp.int32, (nh, 1), 0)
    is_sb = row < H_SB
    is_fox = (row >= H_SB) & (row < H_SB + H_FOX)
    is_moba = row >= H_SB + H_FOX

    @pl.when(n == 0)
    def _():
        acc_ref[...] = jnp.zeros_like(acc_ref)
        m_ref[...] = jnp.where(is_sb, 0.0, NEG) + jnp.zeros_like(m_ref)
        l_ref[...] = jnp.zeros_like(l_ref)
        sbrun_ref[...] = jnp.zeros_like(sbrun_ref)
        lrun_ref[...] = jnp.zeros_like(lrun_ref)
        sg_ref[...] = jnp.full_like(sg_ref, -jnp.inf)
        sm_ref[...] = jnp.full_like(sm_ref, NEG)
        sl_ref[...] = jnp.zeros_like(sl_ref)
        sacc_ref[...] = jnp.zeros_like(sacc_ref)

    q = q_ref[0]
    row2 = lax.broadcasted_iota(jnp.int32, (nh, blk), 0)
    rowd = lax.broadcasted_iota(jnp.int32, (nh, d), 0)

    def head_rows(ref_a, ref_b, h):
        a = ref_a[0, 0, pl.ds(h, page, stride=nh), :]
        b = ref_b[0, 0, pl.ds(h, page, stride=nh), :]
        return jnp.concatenate([a, b], axis=0)

    z = jnp.zeros((nh, blk), F32)
    ksum = jnp.zeros((nh, d), F32)
    for h in range(nh):
        kh = head_rows(ka_ref, kb_ref, h)
        qh = jnp.broadcast_to(q[h:h + 1, :], (8, d))
        sh = _dot_nt(qh, kh.astype(BF16))
        z = jnp.where(row2 == h, jnp.concatenate([sh, sh], axis=0), z)
        if h >= H_SB + H_FOX:
            ks = jnp.broadcast_to(jnp.sum(kh, axis=0, keepdims=True), (nh, d))
            ksum = jnp.where(rowd == h, ks, ksum)
    z = z * ATTN_SCALE
    kmean = (ksum * (1.0 / blk)).astype(BF16).astype(F32)
    gate = jnp.sum(q.astype(F32) * kmean, axis=1, keepdims=True)

    sp = _softplus(z)
    logf = jnp.concatenate([lfa_ref[0, 0], lfb_ref[0, 0]], axis=1)
    x = jnp.where(is_sb, -sp, logf)
    upper = (lax.broadcasted_iota(jnp.int32, (blk, blk), 0) >
             lax.broadcasted_iota(jnp.int32, (blk, blk), 1)).astype(BF16)
    x3 = jnp.concatenate(_split3(x), axis=0)
    s3 = _dot(x3, upper)
    suffix = s3[0:nh] + (s3[nh:2 * nh] + s3[2 * nh:3 * nh])
    xsum = jnp.sum(x, axis=1, keepdims=True)

    sbrun = sbrun_ref[:, 0:1]
    lrun = lrun_ref[:, 0:1]
    m_old = m_ref[:, 0:1]
    l_old = l_ref[:, 0:1]
    s_fox = z + (lfn_ref[0] + suffix + lrun)
    zmax = jnp.max(jnp.where(is_fox, s_fox, z), axis=1, keepdims=True)
    m_fox = jnp.maximum(m_old, zmax)
    arg = jnp.where(is_sb, z - sp + suffix + sbrun, jnp.where(is_fox, s_fox - m_fox, z - zmax))
    p = jnp.exp(arg)
    psum = jnp.sum(p, axis=1, keepdims=True)
    pb = p.astype(BF16)

    pv = jnp.zeros((nh, d), F32)
    for h in range(nh):
        vh = head_rows(va_ref, vb_ref, h).astype(BF16)
        ph = jnp.broadcast_to(pb[h:h + 1, :], (8, blk))
        oh = _dot(ph, vh)
        pv = jnp.where(rowd == h, jnp.concatenate([oh, oh], axis=0), pv)

    m_new = jnp.where(is_fox, m_fox, m_old)
    alpha = jnp.exp(m_old - m_new)
    acc_ref[...] = alpha * acc_ref[...] + pv
    l_ref[...] = jnp.broadcast_to(alpha * l_old + psum, l_ref.shape)
    m_ref[...] = jnp.broadcast_to(m_new, m_ref.shape)
    sbrun_ref[...] = jnp.broadcast_to(sbrun + jnp.where(is_sb, xsum, 0.0), sbrun_ref.shape)
    lrun_ref[...] = jnp.broadcast_to(lrun + jnp.where(is_fox, xsum, 0.0), lrun_ref.shape)

    g = [sg_ref[k][:, 0:1] for k in range(MOBA_TOPK)]
    beats = [gate >= g[k] for k in range(MOBA_TOPK)]

    def insert(ref, new, width):
        old = [ref[k] for k in range(MOBA_TOPK)]
        new = jnp.broadcast_to(new, (nh, width))
        ref[2] = jnp.where(beats[1], old[1], jnp.where(beats[2], new, old[2]))
        ref[1] = jnp.where(beats[0], old[0], jnp.where(beats[1], new, old[1]))
        ref[0] = jnp.where(beats[0], new, old[0])

    insert(sg_ref, gate, LANES)
    insert(sm_ref, zmax, LANES)
    insert(sl_ref, psum, LANES)
    insert(sacc_ref, pv, d)

    @pl.when(n == pl.num_programs(1) - 1)
    def _():
        qf = q.astype(F32)
        kn = kn_ref[0].astype(BF16).astype(F32)
        vn = vn_ref[0].astype(BF16).astype(F32)
        z_new = jnp.sum(qf * kn, axis=1, keepdims=True) * ATTN_SCALE
        acc = acc_ref[...]
        mf = jnp.maximum(m_ref[:, 0:1], z_new)
        af = jnp.exp(m_ref[:, 0:1] - mf)
        pn = jnp.exp(z_new - mf)
        fox_o = (af * acc + pn.astype(BF16).astype(F32) * vn) / (af * l_ref[:, 0:1] + pn)
        mm = z_new
        for k in range(MOBA_TOPK):
            mm = jnp.maximum(mm, jnp.where(sg_ref[k][:, 0:1] > -jnp.inf, sm_ref[k][:, 0:1], NEG))
        pn = jnp.exp(z_new - mm)
        num = pn.astype(BF16).astype(F32) * vn
        den = pn
        for k in range(MOBA_TOPK):
            wk = jnp.where(sg_ref[k][:, 0:1] > -jnp.inf, jnp.exp(sm_ref[k][:, 0:1] - mm), 0.0)
            num = num + wk * sacc_ref[k]
            den = den + wk * sl_ref[k][:, 0:1]
        moba_o = num / den
        o_ref[0] = jnp.where(is_sb, acc, jnp.where(is_fox, fox_o, moba_o))


def _decode_attention(layer, q, k_new, v_new, logf_new, ck, cv, clf, page_table):
    b = q.shape[0]
    n_pages = page_table.shape[1]
    page = clf.shape[3]
    assert n_pages % 2 == 0
    n_blk = n_pages // 2
    nh, d = N_HEADS, HEAD_DIM

    def tok(shape):
        return pl.BlockSpec((1,) + shape, lambda bi, n, pt: (bi, 0, 0))

    def paged(shape, which):
        return pl.BlockSpec((1, 1) + shape,
                            lambda bi, n, pt: (layer, pt[bi, 2 * (n_blk - 1 - n) + which], 0, 0))

    grid_spec = pltpu.PrefetchScalarGridSpec(
        num_scalar_prefetch=1,
        grid=(b, n_blk),
        in_specs=[tok((nh, d)), tok((nh, d)), tok((nh, d)), tok((nh, 1)),
                  paged((page * nh, d), 0), paged((page * nh, d), 1),
                  paged((page * nh, d), 0), paged((page * nh, d), 1),
                  paged((nh, page), 0), paged((nh, page), 1)],
        out_specs=tok((nh, d)),
        scratch_shapes=[pltpu.VMEM((nh, d), F32)] + [pltpu.VMEM((nh, LANES), F32)] * 4 +
                       [pltpu.VMEM((MOBA_TOPK, nh, LANES), F32)] * 3 + [pltpu.VMEM((MOBA_TOPK, nh, d), F32)],
    )
    return pl.pallas_call(
        functools.partial(_decode_kernel, page=page),
        grid_spec=grid_spec,
        out_shape=jax.ShapeDtypeStruct((b, nh, d), F32),
        compiler_params=_cparams("parallel", "arbitrary"),
        name="decode_attention",
    )(page_table, q, k_new, v_new, logf_new, ck, ck, cv, cv, clf, clf)


def _layer_norm(x, g, b):
    mu = jnp.mean(x, axis=-1, keepdims=True)
    var = jnp.mean(jnp.square(x - mu), axis=-1, keepdims=True)
    return (x - mu) * lax.rsqrt(var + LN_EPS) * g + b


def _rope_tables(pos):
    half = ROPE_DIM // 2
    inv = ROPE_THETA ** (-jnp.arange(half, dtype=F32) / half)
    ang = pos.astype(F32)[:, None] * inv
    cos, sin = jnp.cos(ang), jnp.sin(ang)
    t = pos.shape[0]
    one = jnp.ones((t, HEAD_DIM - ROPE_DIM), F32)
    zero = jnp.zeros((t, HEAD_DIM - ROPE_DIM), F32)
    zh = jnp.zeros((t, half), F32)
    c = jnp.concatenate([cos, cos, one], axis=1)
    s_hi = jnp.concatenate([zh, sin, zero], axis=1)
    s_lo = jnp.concatenate([-sin, zh, zero], axis=1)
    return tuple(jnp.tile(a, (1, H_MOBA)) for a in (c, s_hi, s_lo))


def _rope_flat(x, tables):
    c, s_hi, s_lo = tables
    half = ROPE_DIM // 2
    return x * c + jnp.roll(x, half, axis=-1) * s_hi + jnp.roll(x, -half, axis=-1) * s_lo


def _project(x2d, w_in_p, b_f, tm):
    hd = N_HEADS * HEAD_DIM
    p = _mm(x2d, w_in_p, tm=tm, tn=640)
    logf = jax.nn.log_sigmoid(p[:, 3 * hd:3 * hd + H_FOX] + b_f)
    return p[:, :hd], p[:, hd:2 * hd], p[:, 2 * hd:3 * hd], logf


def _merge(o2d, norm_g, w_out_b, tm):
    a, bnd = H_SB * HEAD_DIM, (H_SB + H_FOX) * HEAD_DIM
    parts = []
    for lo, hi in ((0, a), (a, bnd), (bnd, D_MODEL)):
        of = o2d[:, lo:hi]
        parts.append(of * lax.rsqrt(jnp.mean(jnp.square(of), axis=-1, keepdims=True) + RMS_EPS))
    h = jnp.concatenate(parts, axis=-1) * norm_g
    return _mm(h, w_out_b, tm=tm, tn=D_MODEL)


def _heads(x2d, b, t):
    return jnp.transpose(x2d.reshape(b, t, N_HEADS, HEAD_DIM), (0, 2, 1, 3))


def _mix_prompt(x2d, b, t, w_in_p, b_f, norm_g, w_out_b):
    q, k, v, logf = _project(x2d, w_in_p, b_f, tm=1024)
    mo = (H_SB + H_FOX) * HEAD_DIM
    tables = _rope_tables(jnp.arange(t))
    q = jnp.concatenate([q[:, :mo], _rope_flat(q[:, mo:].reshape(b, t, -1), tables).reshape(b * t, -1)], axis=1)
    k = jnp.concatenate([k[:, :mo], _rope_flat(k[:, mo:].reshape(b, t, -1), tables).reshape(b * t, -1)], axis=1)
    qh, kh, vh = (_heads(a, b, t).astype(BF16) for a in (q, k, v))
    s0, s1 = H_SB, H_SB + H_FOX
    c = jnp.cumsum(logf.reshape(b, t, H_FOX), axis=1)
    c_t = jnp.transpose(c, (0, 2, 1)).reshape(b, H_FOX, t // ATT_BLOCK, ATT_BLOCK)
    k_moba = k[:, mo:].reshape(b, t // MOBA_BLOCK, MOBA_BLOCK, H_MOBA, HEAD_DIM)
    k_mean = jnp.transpose(jnp.mean(k_moba, axis=2), (0, 2, 1, 3))
    k_mean = jnp.pad(k_mean, ((0, 0), (0, 0), (0, LANES - k_mean.shape[2]), (0, 0))).astype(BF16)
    oa = _prompt_attention("sb", qh[:, :s0], kh[:, :s0], vh[:, :s0])
    ob = _prompt_attention("fox", qh[:, s0:s1], kh[:, s0:s1], vh[:, s0:s1], c_t)
    oc = _prompt_attention("moba", qh[:, s1:], kh[:, s1:], vh[:, s1:], k_mean)
    o = jnp.transpose(jnp.concatenate([oa, ob, oc], axis=1), (0, 2, 1, 3)).reshape(b * t, D_MODEL)
    y = _merge(o, norm_g, w_out_b, tm=1024)
    return y, k, v, logf


def _mix_sample(layer, x2d, ck, cv, clf, page_table, past, w_in_p, b_f, norm_g, w_out_b):
    b = x2d.shape[0]
    q, k, v, logf = _project(x2d, w_in_p, b_f, tm=b)
    mo = (H_SB + H_FOX) * HEAD_DIM
    tables = _rope_tables(past + jnp.arange(1))
    q = jnp.concatenate([q[:, :mo], _rope_flat(q[:, mo:].reshape(b, 1, -1), tables).reshape(b, -1)], axis=1)
    k = jnp.concatenate([k[:, :mo], _rope_flat(k[:, mo:].reshape(b, 1, -1), tables).reshape(b, -1)], axis=1)
    lf16 = jnp.pad(logf, ((0, 0), (H_SB, H_MOBA)))[:, :, None]
    o = _decode_attention(layer, q.reshape(b, N_HEADS, HEAD_DIM).astype(BF16), k.reshape(b, N_HEADS, HEAD_DIM),
                          v.reshape(b, N_HEADS, HEAD_DIM), lf16, ck, cv, clf, page_table)
    y = _merge(o.reshape(b, D_MODEL), norm_g, w_out_b, tm=b)
    return y, k, v, logf


def _dense_ffn(x2d, w1, w3, w2, tm):
    rows = x2d.shape[0]
    return _ffn(x2d, jnp.zeros((rows // tm,), jnp.int32), jnp.ones((rows, 1), F32), w1, w3, w2, tm, gated=False)


def _moe_ffn(x2d, w_r_p, b_r, w1, w3, w2, tm):
    n, _ = x2d.shape
    logits = _mm(x2d, w_r_p, tm=min(n, 1024), tn=LANES, precise=True)[:, :N_EXPERTS] + b_r
    top_v, top_i = lax.top_k(logits, TOP_K)
    gates = jax.nn.softmax(top_v, axis=-1)
    a = n * TOP_K
    e = top_i.reshape(a)
    order = jnp.argsort(e)
    e_s = e[order]
    counts = jnp.bincount(e, length=N_EXPERTS)
    padded = (counts + tm - 1) // tm * tm
    pad_end = jnp.cumsum(padded)
    pad_start = pad_end - padded
    raw_start = jnp.cumsum(counts) - counts
    dest = (pad_start[e_s] + jnp.arange(a) - raw_start[e_s]).astype(jnp.int32)
    n_blocks = (a + N_EXPERTS * (tm - 1) + tm - 1) // tm
    rows = n_blocks * tm
    tok_buf = jnp.zeros((rows,), jnp.int32).at[dest].set((order // TOP_K).astype(jnp.int32))
    g_buf = jnp.zeros((rows,), F32).at[dest].set(gates.reshape(a)[order])
    blk_expert = jnp.minimum(jnp.searchsorted(pad_end, jnp.arange(n_blocks) * tm, side='right'),
                             N_EXPERTS - 1).astype(jnp.int32)
    y_buf = _ffn(x2d[tok_buf], blk_expert, g_buf[:, None], w1, w3, w2, tm, gated=True)
    where = jnp.zeros((a,), jnp.int32).at[order].set(dest).reshape(n, TOP_K)
    return y_buf[where[:, 0]] + y_buf[where[:, 1]]


def kernel(x_prompt, x_sample, cache_k, cache_v, cache_logf, page_table, w_in, b_f, mix_norm_g, w_out,
           ln1_g, ln1_b, ln2_g, ln2_b, ffn_w1, ffn_w3, ffn_w2, router_w, router_b, moe_w1, moe_w3, moe_w2):
    b, t, d = x_prompt.shape
    bs = x_sample.shape[0]
    depth, pool, page = cache_k.shape[0], cache_k.shape[1], cache_k.shape[2]
    past = page_table.shape[1] * page
    hp = x_prompt.reshape(b * t, d)
    hs = x_sample.reshape(bs, d)
    ck = cache_k.reshape(depth, pool, page * N_HEADS, HEAD_DIM)
    cv = cache_v.reshape(depth, pool, page * N_HEADS, HEAD_DIM)
    clf = jnp.pad(jnp.swapaxes(cache_logf, 2, 3), ((0, 0), (0, 0), (H_SB, H_MOBA), (0, 0)))
    col_pad = (-w_in.shape[2]) % 640
    outs = [[] for _ in range(6)]
    for l in range(depth):
        w_in_p = jnp.pad(w_in[l], ((0, 0), (0, col_pad))).astype(BF16)
        w_out_b = w_out[l].astype(BF16)
        mp, k1, v1, f1 = _mix_prompt(hp, b, t, w_in_p, b_f[l], mix_norm_g[l], w_out_b)
        ms, k2, v2, f2 = _mix_sample(l, hs, ck, cv, clf, page_table, past, w_in_p, b_f[l], mix_norm_g[l], w_out_b)
        hp = _layer_norm(DEEPNORM_ALPHA * hp + mp, ln1_g[l], ln1_b[l])
        hs = _layer_norm(DEEPNORM_ALPHA * hs + ms, ln1_g[l], ln1_b[l])
        j = l // 2
        if l % 2 == 0:
            w1, w3, w2 = (w[j:j + 1].astype(BF16) for w in (ffn_w1, ffn_w3, ffn_w2))
            fp = _dense_ffn(hp, w1, w3, w2, tm=512)
            fs = _dense_ffn(hs, w1, w3, w2, tm=bs)
        else:
            w1, w3, w2 = (w[j].astype(BF16) for w in (moe_w1, moe_w3, moe_w2))
            w_r_p = jnp.pad(router_w[j], ((0, 0), (0, LANES - N_EXPERTS)))
            fp = _moe_ffn(hp, w_r_p, router_b[j], w1, w3, w2, tm=256)
            fs = _moe_ffn(hs, w_r_p, router_b[j], w1, w3, w2, tm=32)
        hp = _layer_norm(DEEPNORM_ALPHA * hp + fp, ln2_g[l], ln2_b[l])
        hs = _layer_norm(DEEPNORM_ALPHA * hs + fs, ln2_g[l], ln2_b[l])
        for lst, val in zip(outs, (k1.reshape(b, t, N_HEADS, HEAD_DIM), v1.reshape(b, t, N_HEADS, HEAD_DIM),
                                   f1.reshape(b, t, H_FOX), k2.reshape(bs, 1, N_HEADS, HEAD_DIM),
                                   v2.reshape(bs, 1, N_HEADS, HEAD_DIM), f2.reshape(bs, 1, H_FOX))):
            lst.append(val)
    return (hp.reshape(b, t, d), hs.reshape(bs, 1, d)) + tuple(jnp.stack(o) for o in outs)
```

```python
import functools

import jax
import jax.numpy as jnp
from jax import lax
from jax.experimental import pallas as pl
from jax.experimental.pallas import tpu as pltpu

F32 = jnp.float32
BF16 = jnp.bfloat16

HEAD_DIM = 64
N_HEADS = 16
H_SB = 4
H_FOX = 6
H_MOBA = 6
D_MODEL = N_HEADS * HEAD_DIM
ROPE_DIM = HEAD_DIM // 4
ROPE_THETA = 500000.0
MOBA_BLOCK = 256
MOBA_TOPK = 3
N_EXPERTS = 8
TOP_K = 2
DEPTH = 2
DEEPNORM_ALPHA = (2 * DEPTH) ** 0.25
ATTN_SCALE = HEAD_DIM ** -0.5
LN_EPS = 1e-5
RMS_EPS = 1e-6

NEG = -1e30
VMEM_LIMIT = 48 * 1024 * 1024
ATT_BLOCK = 256
LANES = 128
ATT_TILES = {"sb": (1024, 256, 1), "fox": (1024, 512, 1), "moba": (1024, 512, 1)}
MOBA_LANES = 16


def _cparams(*sem):
    return pltpu.CompilerParams(dimension_semantics=sem, vmem_limit_bytes=VMEM_LIMIT)


def _dot(a, b):
    return jnp.dot(a, b, preferred_element_type=F32)


def _dot_nt(a, b):
    return lax.dot_general(a, b, (((1,), (1,)), ((), ())), preferred_element_type=F32)


def _split3(x):
    hi = x.astype(BF16)
    r = x - hi.astype(F32)
    mid = r.astype(BF16)
    lo = (r - mid.astype(F32)).astype(BF16)
    return hi, mid, lo


def _softplus(z):
    return jnp.maximum(z, 0.0) + jnp.log1p(jnp.exp(-jnp.abs(z)))


def _mm_kernel(x_ref, w_ref, o_ref, *, precise):
    x = x_ref[...]
    w = w_ref[...]
    if precise:
        xh = x.astype(BF16)
        xl = (x - xh.astype(F32)).astype(BF16)
        wh = w.astype(BF16)
        wl = (w - wh.astype(F32)).astype(BF16)
        o_ref[...] = _dot(xh, wh) + (_dot(xh, wl) + _dot(xl, wh))
    else:
        o_ref[...] = _dot(x.astype(BF16), w.astype(BF16))


def _mm(x, w, tm, tn, precise=False):
    m, k = x.shape
    n = w.shape[1]
    assert m % tm == 0 and n % tn == 0, (x.shape, w.shape, tm, tn)
    return pl.pallas_call(
        functools.partial(_mm_kernel, precise=precise),
        grid=(m // tm, n // tn),
        in_specs=[pl.BlockSpec((tm, k), lambda i, j: (i, 0)),
                  pl.BlockSpec((k, tn), lambda i, j: (0, j))],
        out_specs=pl.BlockSpec((tm, tn), lambda i, j: (i, j)),
        out_shape=jax.ShapeDtypeStruct((m, n), F32),
        compiler_params=_cparams("parallel", "arbitrary"),
        name="matmul",
    )(x, w)


def _ffn_kernel(be_ref, x_ref, g_ref, w1_ref, w3_ref, w2_ref, o_ref, *, tf, gated):
    del be_ref
    x = x_ref[...].astype(BF16)
    d_ff = w1_ref.shape[2]
    for c in range(d_ff // tf):
        sl = slice(c * tf, (c + 1) * tf)
        h1 = _dot(x, w1_ref[0, :, sl])
        h3 = _dot(x, w3_ref[0, :, sl])
        h = (h1 * jax.nn.sigmoid(h1)) * h3
        y = _dot(h.astype(BF16), w2_ref[0, sl, :])
        if c == 0:
            o_ref[...] = y
        else:
            o_ref[...] += y
    if gated:
        o_ref[...] = o_ref[...] * g_ref[...]


def _ffn(x, blk_expert, gate, w1, w3, w2, tm, gated):
    rows, d = x.shape
    d_ff = w1.shape[2]
    tf = 256
    assert rows % tm == 0 and d_ff % tf == 0
    resident = pl.Buffered(1)
    grid_spec = pltpu.PrefetchScalarGridSpec(
        num_scalar_prefetch=1,
        grid=(rows // tm,),
        in_specs=[pl.BlockSpec((tm, d), lambda i, be: (i, 0)),
                  pl.BlockSpec((tm, 1), lambda i, be: (i, 0)),
                  pl.BlockSpec((1, d, d_ff), lambda i, be: (be[i], 0, 0), pipeline_mode=resident),
                  pl.BlockSpec((1, d, d_ff), lambda i, be: (be[i], 0, 0), pipeline_mode=resident),
                  pl.BlockSpec((1, d_ff, d), lambda i, be: (be[i], 0, 0), pipeline_mode=resident)],
        out_specs=pl.BlockSpec((tm, d), lambda i, be: (i, 0)),
    )
    return pl.pallas_call(
        functools.partial(_ffn_kernel, tf=tf, gated=gated),
        grid_spec=grid_spec,
        out_shape=jax.ShapeDtypeStruct((rows, d), F32),
        compiler_params=_cparams("arbitrary"),
        name="swiglu",
    )(blk_expert, x, gate, w1, w3, w2)


def _kv_tile(ref, h, j, tk):
    return ref[0, h, pl.ds(pl.multiple_of(j * tk, tk), tk), :]


def _diag_mask(d, tq, tk, strict):
    rows = lax.broadcasted_iota(jnp.int32, (tq, 1), 0)
    cols = d * tk + lax.broadcasted_iota(jnp.int32, (1, tk), 1)
    return cols < rows if strict else cols <= rows


def _flash_step(s, v, carry):
    acc, m, l = carry
    m_new = jnp.maximum(m, jnp.max(s, axis=1, keepdims=True))
    alpha = jnp.exp(m - m_new)
    p = jnp.exp(s - m_new)
    l = alpha * l + jnp.sum(p, axis=1, keepdims=True)
    acc = alpha * acc + _dot(p.astype(BF16), v)
    return acc, m_new, l


def _moba_query(qp, km, row0):
    tq = qp.shape[0]
    lane = lax.broadcasted_iota(jnp.int32, (tq, LANES), 1)
    own = HEAD_DIM + (row0 + lax.broadcasted_iota(jnp.int32, (tq, 1), 0)) // MOBA_BLOCK
    gate = jnp.where((lane >= HEAD_DIM) & (lane < own), _dot_nt(qp, km), -jnp.inf)
    sel = lane == own
    for _ in range(MOBA_TOPK):
        best = jnp.max(gate, axis=1, keepdims=True)
        first = jnp.min(jnp.where(gate == best, lane, 2 * LANES), axis=1, keepdims=True)
        pick = (lane == first) & (best > -jnp.inf)
        sel = sel | pick
        gate = jnp.where(pick, -jnp.inf, gate)
    band = (lane >= HEAD_DIM) & (lane < HEAD_DIM + MOBA_LANES)
    return jnp.where(band, jnp.where(sel, 0.0, NEG).astype(BF16), qp)


def _softmax_kernel(q_ref, k_ref, v_ref, *rest, tk, moba):
    o_ref = rest[-1]
    i = pl.program_id(2)
    tq = q_ref.shape[2]
    zone = tq // tk
    for h in range(q_ref.shape[1]):
        q = q_ref[0, h]
        if moba:
            q = _moba_query(q, rest[0][0, h], i * tq)
        carry = (jnp.zeros((tq, HEAD_DIM), F32), jnp.full((tq, 1), NEG, F32), jnp.zeros((tq, 1), F32))
        for d in range(zone):
            j = i * zone + d
            s = jnp.where(_diag_mask(d, tq, tk, False), _dot_nt(q, _kv_tile(k_ref, h, j, tk)), NEG)
            carry = _flash_step(s, _kv_tile(v_ref, h, j, tk), carry)

        def body(j, carry, q=q, h=h):
            return _flash_step(_dot_nt(q, _kv_tile(k_ref, h, j, tk)), _kv_tile(v_ref, h, j, tk), carry)

        acc, _, l = lax.fori_loop(0, i * zone, body, carry)
        o_ref[0, h] = acc / l


def _sb_kernel(q_ref, k_ref, v_ref, u_ref, o_ref, *, tk):
    i = pl.program_id(2)
    tq = q_ref.shape[2]
    zone = tq // tk

    def step(q, k, v, carry, mask):
        acc, run = carry
        z = _dot_nt(q, k)
        l1m = -(jnp.maximum(z, 0.0) + jnp.log(1.0 + jnp.exp(-jnp.abs(z))))
        if mask is not None:
            l1m = jnp.where(mask, l1m, 0.0)
        hi = l1m.astype(BF16)
        lo = (l1m - hi.astype(F32)).astype(BF16)
        suffix = _dot(jnp.concatenate([hi, lo], axis=1), u_ref[...])
        w = jnp.exp(z + suffix + run)
        if mask is not None:
            w = jnp.where(mask, w, 0.0)
        acc = acc + _dot(w.astype(BF16), v)
        return acc, run + suffix[:, 0:1]

    for h in range(q_ref.shape[1]):
        q = q_ref[0, h]
        carry = (jnp.zeros((tq, HEAD_DIM), F32), jnp.zeros((tq, 1), F32))
        for d in reversed(range(zone)):
            j = i * zone + d
            carry = step(q, _kv_tile(k_ref, h, j, tk), _kv_tile(v_ref, h, j, tk), carry,
                         _diag_mask(d, tq, tk, True))

        def body(jj, carry, q=q, h=h):
            j = i * zone - 1 - jj
            return step(q, _kv_tile(k_ref, h, j, tk), _kv_tile(v_ref, h, j, tk), carry, None)

        o_ref[0, h] = lax.fori_loop(0, i * zone, body, carry)[0]


def _prompt_attention(kind, q, k, v, extra):
    b, h, t, _ = q.shape
    tq, tk, hp = ATT_TILES[kind]
    assert t % tq == 0 and tq % tk == 0 and h % hp == 0

    def rows_spec(width):
        return pl.BlockSpec((1, hp, tq, width), lambda bi, hi, i: (bi, hi, i, 0))

    def seq_spec(width):
        return pl.BlockSpec((1, hp, t, width), lambda bi, hi, i: (bi, hi, 0, 0))

    in_specs = [rows_spec(q.shape[3]), seq_spec(k.shape[3]), seq_spec(HEAD_DIM)]
    args = [q, k, v]
    if kind == "sb":
        body = functools.partial(_sb_kernel, tk=tk)
        in_specs.append(pl.BlockSpec(extra.shape, lambda bi, hi, i: (0, 0)))
        args.append(extra)
    elif kind == "fox":
        body = functools.partial(_softmax_kernel, tk=tk, moba=False)
    else:
        body = functools.partial(_softmax_kernel, tk=tk, moba=True)
        in_specs.append(pl.BlockSpec((1, hp, LANES, LANES), lambda bi, hi, i: (bi, hi, 0, 0)))
        args.append(extra)
    return pl.pallas_call(
        body,
        grid=(b, h // hp, t // tq),
        in_specs=in_specs,
        out_specs=rows_spec(HEAD_DIM),
        out_shape=jax.ShapeDtypeStruct((b, h, t, HEAD_DIM), F32),
        compiler_params=_cparams("parallel", "parallel", "arbitrary"),
        name=kind + "_prompt",
    )(*args)


def _decode_kernel(pt_ref, q_ref, kn_ref, vn_ref, lfn_ref, ka_ref, kb_ref, va_ref, vb_ref, lfa_ref, lfb_ref,
                   o_ref, acc_ref, m_ref, l_ref, sbrun_ref, lrun_ref, sg_ref, sm_ref, sl_ref, sacc_ref,
                   *, page, n_blk):
    bi = pl.program_id(0)
    n = pl.program_id(1)
    nh, d = N_HEADS, HEAD_DIM
    blk = 2 * page
    row = lax.broadcasted_iota(jnp.int32, (nh, 1), 0)
    is_sb = row < H_SB
    is_fox = (row >= H_SB) & (row < H_SB + H_FOX)

    @pl.when(n == 0)
    def _():
        acc_ref[...] = jnp.zeros_like(acc_ref)
        m_ref[...] = jnp.where(is_sb, 0.0, NEG) + jnp.zeros_like(m_ref)
        l_ref[...] = jnp.zeros_like(l_ref)
        sbrun_ref[...] = jnp.zeros_like(sbrun_ref)
        lrun_ref[...] = jnp.zeros_like(lrun_ref)
        sg_ref[...] = jnp.full_like(sg_ref, -jnp.inf)
        sm_ref[...] = jnp.full_like(sm_ref, NEG)
        sl_ref[...] = jnp.zeros_like(sl_ref)
        sacc_ref[...] = jnp.zeros_like(sacc_ref)

    q = q_ref[0]
    row2 = lax.broadcasted_iota(jnp.int32, (nh, blk), 0)
    rowd = lax.broadcasted_iota(jnp.int32, (nh, d), 0)
    rowl = lax.broadcasted_iota(jnp.int32, (nh, LANES), 0)

    def two(x):
        return jnp.concatenate([x, x], axis=0)

    z = jnp.zeros((nh, blk), F32)
    gate = jnp.zeros((nh, LANES), F32)
    for h in range(nh):
        ka = ka_ref[0, 0, h]
        kb = kb_ref[0, 0, h]
        qh = jnp.broadcast_to(q[h:h + 1, :], (8, d))
        sh = jnp.concatenate([_dot(qh, ka.astype(BF16)), _dot(qh, kb.astype(BF16))], axis=1)
        z = jnp.where(row2 == h, two(sh), z)
        if h >= H_SB + H_FOX:
            kmean = jnp.sum(ka + kb, axis=1, keepdims=True) * (1.0 / blk)
            gh = _dot(qh, jnp.broadcast_to(kmean, (d, LANES)).astype(BF16))
            gate = jnp.where(rowl == h, two(gh), gate)
    z = z * ATTN_SCALE
    gate = gate[:, 0:1]

    ra = pt_ref[bi, 2 * (n_blk - 1 - n)] % 8
    rb = pt_ref[bi, 2 * (n_blk - 1 - n) + 1] % 8
    logf = jnp.zeros((nh, blk), F32)
    for i in range(H_FOX):
        lrow = jnp.concatenate([lfa_ref[0, i, pl.ds(ra, 1), :], lfb_ref[0, i, pl.ds(rb, 1), :]], axis=1)
        logf = jnp.where(row2 == H_SB + i, jnp.broadcast_to(lrow, (nh, blk)), logf)

    sp = _softplus(z)
    x = jnp.where(is_sb, -sp, logf)
    upper = (lax.broadcasted_iota(jnp.int32, (blk, blk), 0) >
             lax.broadcasted_iota(jnp.int32, (blk, blk), 1)).astype(BF16)
    x3 = jnp.concatenate(_split3(x), axis=0)
    s3 = _dot(x3, upper)
    suffix = s3[0:nh] + (s3[nh:2 * nh] + s3[2 * nh:3 * nh])
    xsum = jnp.sum(x, axis=1, keepdims=True)

    sbrun = sbrun_ref[:, 0:1]
    lrun = lrun_ref[:, 0:1]
    m_old = m_ref[:, 0:1]
    l_old = l_ref[:, 0:1]
    s_fox = z + (lfn_ref[0] + suffix + lrun)
    zmax = jnp.max(jnp.where(is_fox, s_fox, z), axis=1, keepdims=True)
    m_fox = jnp.maximum(m_old, zmax)
    arg = jnp.where(is_sb, z - sp + suffix + sbrun, jnp.where(is_fox, s_fox - m_fox, z - zmax))
    p = jnp.exp(arg)
    psum = jnp.sum(p, axis=1, keepdims=True)
    pb = p.astype(BF16)

    pv = jnp.zeros((nh, d), F32)
    for h in range(nh):
        ph = jnp.broadcast_to(pb[h:h + 1, :], (8, blk))
        oh = (_dot_nt(ph[:, :page], va_ref[0, 0, h].astype(BF16)) +
              _dot_nt(ph[:, page:], vb_ref[0, 0, h].astype(BF16)))
        pv = jnp.where(rowd == h, two(oh), pv)

    m_new = jnp.where(is_fox, m_fox, m_old)
    alpha = jnp.exp(m_old - m_new)
    acc_ref[...] = alpha * acc_ref[...] + pv
    l_ref[...] = jnp.broadcast_to(alpha * l_old + psum, l_ref.shape)
    m_ref[...] = jnp.broadcast_to(m_new, m_ref.shape)
    sbrun_ref[...] = jnp.broadcast_to(sbrun + jnp.where(is_sb, xsum, 0.0), sbrun_ref.shape)
    lrun_ref[...] = jnp.broadcast_to(lrun + jnp.where(is_fox, xsum, 0.0), lrun_ref.shape)

    g = [sg_ref[k][:, 0:1] for k in range(MOBA_TOPK)]
    beats = [gate >= g[k] for k in range(MOBA_TOPK)]

    def insert(ref, new, width):
        old = [ref[k] for k in range(MOBA_TOPK)]
        new = jnp.broadcast_to(new, (nh, width))
        ref[2] = jnp.where(beats[1], old[1], jnp.where(beats[2], new, old[2]))
        ref[1] = jnp.where(beats[0], old[0], jnp.where(beats[1], new, old[1]))
        ref[0] = jnp.where(beats[0], new, old[0])

    insert(sg_ref, gate, LANES)
    insert(sm_ref, zmax, LANES)
    insert(sl_ref, psum, LANES)
    insert(sacc_ref, pv, d)

    @pl.when(n == n_blk - 1)
    def _():
        qf = q.astype(F32)
        kn = kn_ref[0].astype(BF16).astype(F32)
        vn = vn_ref[0].astype(BF16).astype(F32)
        z_new = jnp.sum(qf * kn, axis=1, keepdims=True) * ATTN_SCALE
        acc = acc_ref[...]
        mf = jnp.maximum(m_ref[:, 0:1], z_new)
        af = jnp.exp(m_ref[:, 0:1] - mf)
        pn = jnp.exp(z_new - mf)
        fox_o = (af * acc + pn.astype(BF16).astype(F32) * vn) / (af * l_ref[:, 0:1] + pn)
        mm = z_new
        for k in range(MOBA_TOPK):
            mm = jnp.maximum(mm, jnp.where(sg_ref[k][:, 0:1] > -jnp.inf, sm_ref[k][:, 0:1], NEG))
        pn = jnp.exp(z_new - mm)
        num = pn.astype(BF16).astype(F32) * vn
        den = pn
        for k in range(MOBA_TOPK):
            wk = jnp.where(sg_ref[k][:, 0:1] > -jnp.inf, jnp.exp(sm_ref[k][:, 0:1] - mm), 0.0)
            num = num + wk * sacc_ref[k]
            den = den + wk * sl_ref[k][:, 0:1]
        moba_o = num / den
        o_ref[0] = jnp.where(is_sb, acc, jnp.where(is_fox, fox_o, moba_o))


def _decode_attention(layer, q, k_new, v_new, logf_new, ck, cv, clf, page_table):
    b = q.shape[0]
    n_pages = page_table.shape[1]
    page = ck.shape[4]
    assert n_pages % 2 == 0 and clf.shape[2] % 8 == 0
    n_blk = n_pages // 2
    nh, d = N_HEADS, HEAD_DIM

    def tok(shape):
        return pl.BlockSpec((1,) + shape, lambda bi, n, pt: (bi, 0, 0))

    def page_of(bi, n, pt, which):
        return pt[bi, 2 * (n_blk - 1 - n) + which]

    def paged(which):
        return pl.BlockSpec((1, 1, nh, d, page), lambda bi, n, pt: (layer, page_of(bi, n, pt, which), 0, 0, 0))

    def paged_logf(which):
        return pl.BlockSpec((1, H_FOX, 8, page), lambda bi, n, pt: (layer, 0, page_of(bi, n, pt, which) // 8, 0))

    grid_spec = pltpu.PrefetchScalarGridSpec(
        num_scalar_prefetch=1,
        grid=(b, n_blk),
        in_specs=[tok((nh, d)), tok((nh, d)), tok((nh, d)), tok((nh, 1)),
                  paged(0), paged(1), paged(0), paged(1), paged_logf(0), paged_logf(1)],
        out_specs=tok((nh, d)),
        scratch_shapes=[pltpu.VMEM((nh, d), F32)] + [pltpu.VMEM((nh, LANES), F32)] * 4 +
                       [pltpu.VMEM((MOBA_TOPK, nh, LANES), F32)] * 3 + [pltpu.VMEM((MOBA_TOPK, nh, d), F32)],
    )
    return pl.pallas_call(
        functools.partial(_decode_kernel, page=page, n_blk=n_blk),
        grid_spec=grid_spec,
        out_shape=jax.ShapeDtypeStruct((b, nh, d), F32),
        compiler_params=_cparams("parallel", "arbitrary"),
        name="decode_attention",
    )(page_table, q, k_new, v_new, logf_new, ck, ck, cv, cv, clf, clf)


def _layer_norm(x, g, b):
    mu = jnp.mean(x, axis=-1, keepdims=True)
    var = jnp.mean(jnp.square(x - mu), axis=-1, keepdims=True)
    return (x - mu) * lax.rsqrt(var + LN_EPS) * g + b


def _rope_tables(pos):
    half = ROPE_DIM // 2
    inv = ROPE_THETA ** (-jnp.arange(half, dtype=F32) / half)
    ang = pos.astype(F32)[:, None] * inv
    cos, sin = jnp.cos(ang), jnp.sin(ang)
    t = pos.shape[0]
    one = jnp.ones((t, HEAD_DIM - ROPE_DIM), F32)
    zero = jnp.zeros((t, HEAD_DIM - ROPE_DIM), F32)
    zh = jnp.zeros((t, half), F32)
    c = jnp.concatenate([cos, cos, one], axis=1)
    s_hi = jnp.concatenate([zh, sin, zero], axis=1)
    s_lo = jnp.concatenate([-sin, zh, zero], axis=1)
    return tuple(jnp.tile(a, (1, H_MOBA)) for a in (c, s_hi, s_lo))


def _rope_flat(x, tables):
    c, s_hi, s_lo = tables
    half = ROPE_DIM // 2
    return x * c + jnp.roll(x, half, axis=-1) * s_hi + jnp.roll(x, -half, axis=-1) * s_lo


def _project(x2d, w_in_p, b_f, tm):
    hd = N_HEADS * HEAD_DIM
    p = _mm(x2d, w_in_p, tm=tm, tn=640)
    logf = jax.nn.log_sigmoid(p[:, 3 * hd:3 * hd + H_FOX] + b_f)
    return p[:, :hd], p[:, hd:2 * hd], p[:, 2 * hd:3 * hd], logf


def _merge(o2d, norm_g, w_out_b, tm):
    a, bnd = H_SB * HEAD_DIM, (H_SB + H_FOX) * HEAD_DIM
    parts = []
    for lo, hi in ((0, a), (a, bnd), (bnd, D_MODEL)):
        of = o2d[:, lo:hi]
        parts.append(of * lax.rsqrt(jnp.mean(jnp.square(of), axis=-1, keepdims=True) + RMS_EPS))
    h = jnp.concatenate(parts, axis=-1) * norm_g
    return _mm(h, w_out_b, tm=tm, tn=D_MODEL)


def _heads(x2d, b, t):
    return jnp.transpose(x2d.reshape(b, t, N_HEADS, HEAD_DIM), (0, 2, 1, 3))


def _mix_prompt(x2d, b, t, w_in_p, b_f, norm_g, w_out_b):
    q, k, v, logf = _project(x2d, w_in_p, b_f, tm=1024)
    mo = (H_SB + H_FOX) * HEAD_DIM
    nb = t // MOBA_BLOCK
    assert nb <= MOBA_LANES
    tables = _rope_tables(jnp.arange(t))
    q = jnp.concatenate([q[:, :mo], _rope_flat(q[:, mo:].reshape(b, t, -1), tables).reshape(b * t, -1)], axis=1)
    k = jnp.concatenate([k[:, :mo], _rope_flat(k[:, mo:].reshape(b, t, -1), tables).reshape(b * t, -1)], axis=1)
    qh = _heads(q * ATTN_SCALE, b, t).astype(BF16)
    kh = _heads(k, b, t).astype(BF16)
    vh = _heads(v, b, t).astype(BF16)
    s0, s1 = H_SB, H_SB + H_FOX
    sb_tk = ATT_TILES["sb"][1]
    tri = jnp.arange(sb_tk)[:, None] >= jnp.arange(sb_tk)[None, :]
    u2 = jnp.concatenate([tri, tri], axis=0).astype(BF16)
    oa = _prompt_attention("sb", qh[:, :s0], kh[:, :s0], vh[:, :s0], u2)
    neg_c = -jnp.transpose(jnp.cumsum(logf.reshape(b, t, H_FOX), axis=1), (0, 2, 1))
    fill = jnp.zeros((b, H_FOX, t, LANES - HEAD_DIM - 3), BF16)
    k_fox = jnp.concatenate([kh[:, s0:s1]] + [p[..., None] for p in _split3(neg_c)] + [fill], axis=-1)
    q_fox = jnp.concatenate([qh[:, s0:s1], jnp.ones((b, H_FOX, t, 3), BF16), fill], axis=-1)
    ob = _prompt_attention("fox", q_fox, k_fox, vh[:, s0:s1], None)
    onehot = (jnp.arange(t)[:, None] // MOBA_BLOCK == jnp.arange(MOBA_LANES)[None, :]).astype(BF16)
    k_mo = jnp.concatenate([kh[:, s1:], jnp.broadcast_to(onehot, (b, H_MOBA, t, MOBA_LANES)),
                            jnp.zeros((b, H_MOBA, t, LANES - HEAD_DIM - MOBA_LANES), BF16)], axis=-1)
    q_mo = jnp.pad(qh[:, s1:], ((0, 0), (0, 0), (0, 0), (0, LANES - HEAD_DIM)))
    k_mean = jnp.mean(k[:, mo:].reshape(b, nb, MOBA_BLOCK, H_MOBA, HEAD_DIM), axis=2)
    k_mean = jnp.transpose(k_mean, (0, 2, 1, 3)).astype(BF16)
    km = jnp.pad(k_mean, ((0, 0), (0, 0), (HEAD_DIM, LANES - HEAD_DIM - nb), (0, LANES - HEAD_DIM)))
    oc = _prompt_attention("moba", q_mo, k_mo, vh[:, s1:], km)
    o = jnp.transpose(jnp.concatenate([oa, ob, oc], axis=1), (0, 2, 1, 3)).reshape(b * t, D_MODEL)
    y = _merge(o, norm_g, w_out_b, tm=1024)
    return y, k, v, logf


def _mix_sample(layer, x2d, ck, cv, clf, page_table, past, w_in_p, b_f, norm_g, w_out_b):
    b = x2d.shape[0]
    q, k, v, logf = _project(x2d, w_in_p, b_f, tm=b)
    mo = (H_SB + H_FOX) * HEAD_DIM
    tables = _rope_tables(past + jnp.arange(1))
    q = jnp.concatenate([q[:, :mo], _rope_flat(q[:, mo:].reshape(b, 1, -1), tables).reshape(b, -1)], axis=1)
    k = jnp.concatenate([k[:, :mo], _rope_flat(k[:, mo:].reshape(b, 1, -1), tables).reshape(b, -1)], axis=1)
    lf16 = jnp.pad(logf, ((0, 0), (H_SB, H_MOBA)))[:, :, None]
    o = _decode_attention(layer, q.reshape(b, N_HEADS, HEAD_DIM).astype(BF16), k.reshape(b, N_HEADS, HEAD_DIM),
                          v.reshape(b, N_HEADS, HEAD_DIM), lf16, ck, cv, clf, page_table)
    y = _merge(o.reshape(b, D_MODEL), norm_g, w_out_b, tm=b)
    return y, k, v, logf


def _dense_ffn(x2d, w1, w3, w2, tm):
    rows = x2d.shape[0]
    return _ffn(x2d, jnp.zeros((rows // tm,), jnp.int32), jnp.ones((rows, 1), F32), w1, w3, w2, tm, gated=False)


def _moe_ffn(x2d, w_r_p, b_r, w1, w3, w2, tm):
    n, _ = x2d.shape
    logits = _mm(x2d, w_r_p, tm=min(n, 1024), tn=LANES, precise=True)[:, :N_EXPERTS] + b_r
    top_v, top_i = lax.top_k(logits, TOP_K)
    gates = jax.nn.softmax(top_v, axis=-1)
    a = n * TOP_K
    e = top_i.reshape(a)
    order = jnp.argsort(e)
    e_s = e[order]
    counts = jnp.bincount(e, length=N_EXPERTS)
    padded = (counts + tm - 1) // tm * tm
    pad_end = jnp.cumsum(padded)
    pad_start = pad_end - padded
    raw_start = jnp.cumsum(counts) - counts
    dest = (pad_start[e_s] + jnp.arange(a) - raw_start[e_s]).astype(jnp.int32)
    n_blocks = (a + N_EXPERTS * (tm - 1) + tm - 1) // tm
    rows = n_blocks * tm
    tok_buf = jnp.zeros((rows,), jnp.int32).at[dest].set((order // TOP_K).astype(jnp.int32))
    g_buf = jnp.zeros((rows,), F32).at[dest].set(gates.reshape(a)[order])
    blk_expert = jnp.minimum(jnp.searchsorted(pad_end, jnp.arange(n_blocks) * tm, side='right'),
                             N_EXPERTS - 1).astype(jnp.int32)
    y_buf = _ffn(x2d[tok_buf], blk_expert, g_buf[:, None], w1, w3, w2, tm, gated=True)
    where = jnp.zeros((a,), jnp.int32).at[order].set(dest).reshape(n, TOP_K)
    return y_buf[where[:, 0]] + y_buf[where[:, 1]]


def kernel(x_prompt, x_sample, cache_k, cache_v, cache_logf, page_table, w_in, b_f, mix_norm_g, w_out,
           ln1_g, ln1_b, ln2_g, ln2_b, ffn_w1, ffn_w3, ffn_w2, router_w, router_b, moe_w1, moe_w3, moe_w2):
    b, t, d = x_prompt.shape
    bs = x_sample.shape[0]
    depth, pool, page = cache_k.shape[0], cache_k.shape[1], cache_k.shape[2]
    past = page_table.shape[1] * page
    hp = x_prompt.reshape(b * t, d)
    hs = x_sample.reshape(bs, d)
    ck = jnp.transpose(cache_k, (0, 1, 3, 4, 2))
    cv = jnp.transpose(cache_v, (0, 1, 3, 4, 2))
    clf = jnp.transpose(cache_logf, (0, 3, 1, 2))
    col_pad = (-w_in.shape[2]) % 640
    outs = [[] for _ in range(6)]
    for l in range(depth):
        w_in_p = jnp.pad(w_in[l], ((0, 0), (0, col_pad))).astype(BF16)
        w_out_b = w_out[l].astype(BF16)
        mp, k1, v1, f1 = _mix_prompt(hp, b, t, w_in_p, b_f[l], mix_norm_g[l], w_out_b)
        ms, k2, v2, f2 = _mix_sample(l, hs, ck, cv, clf, page_table, past, w_in_p, b_f[l], mix_norm_g[l], w_out_b)
        hp = _layer_norm(DEEPNORM_ALPHA * hp + mp, ln1_g[l], ln1_b[l])
        hs = _layer_norm(DEEPNORM_ALPHA * hs + ms, ln1_g[l], ln1_b[l])
        j = l // 2
        if l % 2 == 0:
            w1, w3, w2 = (w[j:j + 1].astype(BF16) for w in (ffn_w1, ffn_w3, ffn_w2))
            fp = _dense_ffn(hp, w1, w3, w2, tm=512)
            fs = _dense_ffn(hs, w1, w3, w2, tm=bs)
        else:
            w1, w3, w2 = (w[j].astype(BF16) for w in (moe_w1, moe_w3, moe_w2))
            w_r_p = jnp.pad(router_w[j], ((0, 0), (0, LANES - N_EXPERTS)))
            fp = _moe_ffn(hp, w_r_p, router_b[j], w1, w3, w2, tm=256)
            fs = _moe_ffn(hs, w_r_p, router_b[j], w1, w3, w2, tm=32)
        hp = _layer_norm(DEEPNORM_ALPHA * hp + fp, ln2_g[l], ln2_b[l])
        hs = _layer_norm(DEEPNORM_ALPHA * hs + fs, ln2_g[l], ln2_b[l])
        for lst, val in zip(outs, (k1.reshape(b, t, N_HEADS, HEAD_DIM), v1.reshape(b, t, N_HEADS, HEAD_DIM),
                                   f1.reshape(b, t, H_FOX), k2.reshape(bs, 1, N_HEADS, HEAD_DIM),
                                   v2.reshape(bs, 1, N_HEADS, HEAD_DIM), f2.reshape(bs, 1, H_FOX))):
            lst.append(val)
    return (hp.reshape(b, t, d), hs.reshape(bs, 1, d)) + tuple(jnp.stack(o) for o in outs)
```

```python
import functools

import jax
import jax.numpy as jnp
from jax import lax
from jax.experimental import pallas as pl
from jax.experimental.pallas import tpu as pltpu

F32 = jnp.float32
BF16 = jnp.bfloat16

HEAD_DIM = 64
N_HEADS = 16
H_SB = 4
H_FOX = 6
H_MOBA = 6
D_MODEL = N_HEADS * HEAD_DIM
ROPE_DIM = HEAD_DIM // 4
ROPE_THETA = 500000.0
MOBA_BLOCK = 256
MOBA_TOPK = 3
N_EXPERTS = 8
TOP_K = 2
DEPTH = 2
DEEPNORM_ALPHA = (2 * DEPTH) ** 0.25
ATTN_SCALE = HEAD_DIM ** -0.5
LN_EPS = 1e-5
RMS_EPS = 1e-6

NEG = -1e30
VMEM_LIMIT = 48 * 1024 * 1024
LANES = 128
ATT_TILES = {"sb": (1024, 256, 1), "fox": (1024, 512, 1), "moba": (1024, 512, 1)}
DECODE_PAGES = 8
MOBA_LANES = 16


def _cparams(*sem):
    return pltpu.CompilerParams(dimension_semantics=sem, vmem_limit_bytes=VMEM_LIMIT)


def _dot(a, b):
    return jnp.dot(a, b, preferred_element_type=F32)


def _dot_nt(a, b):
    return lax.dot_general(a, b, (((1,), (1,)), ((), ())), preferred_element_type=F32)


def _split3(x):
    hi = x.astype(BF16)
    r = x - hi.astype(F32)
    mid = r.astype(BF16)
    lo = (r - mid.astype(F32)).astype(BF16)
    return hi, mid, lo


def _softplus(z):
    return jnp.maximum(z, 0.0) + jnp.log1p(jnp.exp(-jnp.abs(z)))


def _mm_kernel(x_ref, w_ref, o_ref, *, precise):
    x = x_ref[...]
    w = w_ref[...]
    if precise:
        xh = x.astype(BF16)
        xl = (x - xh.astype(F32)).astype(BF16)
        wh = w.astype(BF16)
        wl = (w - wh.astype(F32)).astype(BF16)
        o_ref[...] = _dot(xh, wh) + (_dot(xh, wl) + _dot(xl, wh))
    else:
        o_ref[...] = _dot(x.astype(BF16), w.astype(BF16))


def _mm(x, w, tm, tn, precise=False):
    m, k = x.shape
    n = w.shape[1]
    assert m % tm == 0 and n % tn == 0, (x.shape, w.shape, tm, tn)
    return pl.pallas_call(
        functools.partial(_mm_kernel, precise=precise),
        grid=(m // tm, n // tn),
        in_specs=[pl.BlockSpec((tm, k), lambda i, j: (i, 0)),
                  pl.BlockSpec((k, tn), lambda i, j: (0, j))],
        out_specs=pl.BlockSpec((tm, tn), lambda i, j: (i, j)),
        out_shape=jax.ShapeDtypeStruct((m, n), F32),
        compiler_params=_cparams("parallel", "arbitrary"),
        name="matmul",
    )(x, w)


def _ffn_kernel(be_ref, x_ref, w1_ref, w3_ref, w2_ref, o_ref, *, tf):
    del be_ref
    x = x_ref[...].astype(BF16)
    d_ff = w1_ref.shape[2]
    for c in range(d_ff // tf):
        sl = slice(c * tf, (c + 1) * tf)
        h1 = _dot(x, w1_ref[0, :, sl])
        h3 = _dot(x, w3_ref[0, :, sl])
        h = (h1 * jax.nn.sigmoid(h1)) * h3
        y = _dot(h.astype(BF16), w2_ref[0, sl, :])
        if c == 0:
            o_ref[...] = y
        else:
            o_ref[...] += y


def _ffn(x, blk_expert, w1, w3, w2, tm):
    rows, d = x.shape
    d_ff = w1.shape[2]
    tf = 256
    assert rows % tm == 0 and d_ff % tf == 0
    resident = pl.Buffered(1)
    grid_spec = pltpu.PrefetchScalarGridSpec(
        num_scalar_prefetch=1,
        grid=(rows // tm,),
        in_specs=[pl.BlockSpec((tm, d), lambda i, be: (i, 0)),
                  pl.BlockSpec((1, d, d_ff), lambda i, be: (be[i], 0, 0), pipeline_mode=resident),
                  pl.BlockSpec((1, d, d_ff), lambda i, be: (be[i], 0, 0), pipeline_mode=resident),
                  pl.BlockSpec((1, d_ff, d), lambda i, be: (be[i], 0, 0), pipeline_mode=resident)],
        out_specs=pl.BlockSpec((tm, d), lambda i, be: (i, 0)),
    )
    return pl.pallas_call(
        functools.partial(_ffn_kernel, tf=tf),
        grid_spec=grid_spec,
        out_shape=jax.ShapeDtypeStruct((rows, d), F32),
        compiler_params=_cparams("arbitrary"),
        name="swiglu",
    )(blk_expert, x, w1, w3, w2)


def _kv_tile(ref, h, j, tk):
    return ref[0, h, pl.ds(pl.multiple_of(j * tk, tk), tk), :]


def _diag_mask(d, tq, tk, strict):
    rows = lax.broadcasted_iota(jnp.int32, (tq, 1), 0)
    cols = d * tk + lax.broadcasted_iota(jnp.int32, (1, tk), 1)
    return cols < rows if strict else cols <= rows


def _flash_step(s, v, carry):
    acc, m, l = carry
    m_new = jnp.maximum(m, jnp.max(s, axis=1, keepdims=True))
    alpha = jnp.exp(m - m_new)
    p = jnp.exp(s - m_new)
    l = alpha * l + jnp.sum(p, axis=1, keepdims=True)
    acc = alpha * acc + _dot(p.astype(BF16), v)
    return acc, m_new, l


def _moba_query(qp, km, row0):
    tq = qp.shape[0]
    lane = lax.broadcasted_iota(jnp.int32, (tq, LANES), 1)
    own = HEAD_DIM + (row0 + lax.broadcasted_iota(jnp.int32, (tq, 1), 0)) // MOBA_BLOCK
    gate = jnp.where((lane >= HEAD_DIM) & (lane < own), _dot_nt(qp, km), -jnp.inf)
    sel = lane == own
    for _ in range(MOBA_TOPK):
        best = jnp.max(gate, axis=1, keepdims=True)
        first = jnp.min(jnp.where(gate == best, lane, 2 * LANES), axis=1, keepdims=True)
        pick = (lane == first) & (best > -jnp.inf)
        sel = sel | pick
        gate = jnp.where(pick, -jnp.inf, gate)
    band = (lane >= HEAD_DIM) & (lane < HEAD_DIM + MOBA_LANES)
    return jnp.where(band, jnp.where(sel, 0.0, NEG).astype(BF16), qp)


def _softmax_kernel(q_ref, k_ref, v_ref, *rest, tk, moba):
    o_ref = rest[-1]
    i = pl.program_id(2)
    tq = q_ref.shape[2]
    zone = tq // tk
    for h in range(q_ref.shape[1]):
        q = q_ref[0, h]
        if moba:
            q = _moba_query(q, rest[0][0, h], i * tq)
        carry = (jnp.zeros((tq, HEAD_DIM), F32), jnp.full((tq, 1), NEG, F32), jnp.zeros((tq, 1), F32))
        for d in range(zone):
            j = i * zone + d
            s = jnp.where(_diag_mask(d, tq, tk, False), _dot_nt(q, _kv_tile(k_ref, h, j, tk)), NEG)
            carry = _flash_step(s, _kv_tile(v_ref, h, j, tk), carry)

        def body(j, carry, q=q, h=h):
            return _flash_step(_dot_nt(q, _kv_tile(k_ref, h, j, tk)), _kv_tile(v_ref, h, j, tk), carry)

        acc, _, l = lax.fori_loop(0, i * zone, body, carry)
        o_ref[0, h] = acc / l


def _sb_kernel(q_ref, k_ref, v_ref, u_ref, o_ref, *, tk):
    i = pl.program_id(2)
    tq = q_ref.shape[2]
    zone = tq // tk

    def step(q, k, v, carry, mask):
        acc, run = carry
        z = _dot_nt(q, k)
        l1m = -(jnp.maximum(z, 0.0) + jnp.log(1.0 + jnp.exp(-jnp.abs(z))))
        if mask is not None:
            l1m = jnp.where(mask, l1m, 0.0)
        hi = l1m.astype(BF16)
        lo = (l1m - hi.astype(F32)).astype(BF16)
        suffix = _dot(jnp.concatenate([hi, lo], axis=1), u_ref[...])
        w = jnp.exp(z + suffix + run)
        if mask is not None:
            w = jnp.where(mask, w, 0.0)
        acc = acc + _dot(w.astype(BF16), v)
        return acc, run + suffix[:, 0:1]

    for h in range(q_ref.shape[1]):
        q = q_ref[0, h]
        carry = (jnp.zeros((tq, HEAD_DIM), F32), jnp.zeros((tq, 1), F32))
        for d in reversed(range(zone)):
            j = i * zone + d
            carry = step(q, _kv_tile(k_ref, h, j, tk), _kv_tile(v_ref, h, j, tk), carry,
                         _diag_mask(d, tq, tk, True))

        def body(jj, carry, q=q, h=h):
            j = i * zone - 1 - jj
            return step(q, _kv_tile(k_ref, h, j, tk), _kv_tile(v_ref, h, j, tk), carry, None)

        o_ref[0, h] = lax.fori_loop(0, i * zone, body, carry)[0]


def _prompt_attention(kind, q, k, v, extra):
    b, h, t, _ = q.shape
    tq, tk, hp = ATT_TILES[kind]
    assert t % tq == 0 and tq % tk == 0 and h % hp == 0

    def rows_spec(width):
        return pl.BlockSpec((1, hp, tq, width), lambda bi, hi, i: (bi, hi, i, 0))

    def seq_spec(width):
        return pl.BlockSpec((1, hp, t, width), lambda bi, hi, i: (bi, hi, 0, 0))

    in_specs = [rows_spec(q.shape[3]), seq_spec(k.shape[3]), seq_spec(HEAD_DIM)]
    args = [q, k, v]
    if kind == "sb":
        body = functools.partial(_sb_kernel, tk=tk)
        in_specs.append(pl.BlockSpec(extra.shape, lambda bi, hi, i: (0, 0)))
        args.append(extra)
    elif kind == "fox":
        body = functools.partial(_softmax_kernel, tk=tk, moba=False)
    else:
        body = functools.partial(_softmax_kernel, tk=tk, moba=True)
        in_specs.append(pl.BlockSpec((1, hp, LANES, LANES), lambda bi, hi, i: (bi, hi, 0, 0)))
        args.append(extra)
    return pl.pallas_call(
        body,
        grid=(b, h // hp, t // tq),
        in_specs=in_specs,
        out_specs=rows_spec(HEAD_DIM),
        out_shape=jax.ShapeDtypeStruct((b, h, t, HEAD_DIM), F32),
        compiler_params=_cparams("parallel", "parallel", "arbitrary"),
        name=kind + "_prompt",
    )(*args)


def _decode_kernel(pt_ref, q_ref, qbd_ref, kn_ref, vn_ref, lfn_ref, fold_ref, *refs, page, n_steps, pages):
    k_refs, v_refs, lf_refs = refs[:pages], refs[pages:2 * pages], refs[2 * pages:3 * pages]
    o_ref, acc_ref, m_ref, l_ref, sbrun_ref, lrun_ref, sg_ref, sm_ref, sl_ref, sacc_ref = refs[3 * pages:]
    bi = pl.program_id(0)
    n = pl.program_id(1)
    nh, d = N_HEADS, HEAD_DIM
    blk = 2 * page
    nb = pages // 2
    row = lax.broadcasted_iota(jnp.int32, (nh, 1), 0)
    is_sb = row < H_SB
    is_fox = (row >= H_SB) & (row < H_SB + H_FOX)

    @pl.when(n == 0)
    def _():
        acc_ref[...] = jnp.zeros_like(acc_ref)
        m_ref[...] = jnp.where(is_sb, 0.0, NEG) + jnp.zeros_like(m_ref)
        l_ref[...] = jnp.zeros_like(l_ref)
        sbrun_ref[...] = jnp.zeros_like(sbrun_ref)
        lrun_ref[...] = jnp.zeros_like(lrun_ref)
        sg_ref[...] = jnp.full_like(sg_ref, -jnp.inf)
        sm_ref[...] = jnp.full_like(sm_ref, NEG)
        sl_ref[...] = jnp.zeros_like(sl_ref)
        sacc_ref[...] = jnp.zeros_like(sacc_ref)

    q = q_ref[0]
    qbd = qbd_ref[0]
    rowp = lax.broadcasted_iota(jnp.int32, (nh, page), 0)
    own_cols = lax.broadcasted_iota(jnp.int32, (nh, nh * d), 1) // d == lax.broadcasted_iota(
        jnp.int32, (nh, nh * d), 0)
    mo = (H_SB + H_FOX) * d
    lane_m = lax.broadcasted_iota(jnp.int32, (nh * d - mo, LANES), 1)

    zs, lfs = [], []
    kmeans = jnp.zeros((nh * d - mo, LANES), F32)
    for s in range(pages):
        ks = k_refs[s][0, 0]
        zs.append(_dot(qbd, ks.astype(BF16)) * ATTN_SCALE)
        if s % 2 == 0:
            first = ks[mo:]
        else:
            kmean = jnp.sum(first + ks[mo:], axis=1, keepdims=True) * (1.0 / blk)
            kmeans = jnp.where(lane_m == s // 2, jnp.broadcast_to(kmean, kmeans.shape), kmeans)
        r = pt_ref[bi, pages * (n_steps - 1 - n) + s] % 8
        lf = jnp.zeros((nh, page), F32)
        for i in range(H_FOX):
            lf = jnp.where(rowp == H_SB + i, jnp.broadcast_to(lf_refs[s][0, i, pl.ds(r, 1), :], (nh, page)), lf)
        lfs.append(lf)
    gates = _dot(qbd[:, mo:], kmeans.astype(BF16))

    upper = (lax.broadcasted_iota(jnp.int32, (blk, blk), 0) >
             lax.broadcasted_iota(jnp.int32, (blk, blk), 1)).astype(BF16)
    zb = [jnp.concatenate([zs[2 * b], zs[2 * b + 1]], axis=1) for b in range(nb)]
    spb = [_softplus(z) for z in zb]
    xb = [jnp.where(is_sb, -spb[b], jnp.concatenate([lfs[2 * b], lfs[2 * b + 1]], axis=1)) for b in range(nb)]
    sufb = []
    for b in range(nb):
        s3 = _dot(jnp.concatenate(_split3(xb[b]), axis=0), upper)
        sufb.append(s3[0:nh] + (s3[nh:2 * nh] + s3[2 * nh:3 * nh]))
    ahead = [None] * nb
    total = jnp.zeros((nh, 1), F32)
    for b in reversed(range(nb)):
        ahead[b] = total
        total = total + jnp.sum(xb[b], axis=1, keepdims=True)

    sbrun = sbrun_ref[:, 0:1]
    lrun = lrun_ref[:, 0:1]
    m_old = m_ref[:, 0:1]
    l_old = l_ref[:, 0:1]
    s_fox = [zb[b] + (lfn_ref[0] + sufb[b] + (lrun + ahead[b])) for b in range(nb)]
    zmax = [jnp.max(jnp.where(is_fox, s_fox[b], zb[b]), axis=1, keepdims=True) for b in range(nb)]
    m_fox = m_old
    for b in range(nb):
        m_fox = jnp.maximum(m_fox, zmax[b])
    psum, pv = [], []
    for b in range(nb):
        arg = jnp.where(is_sb, zb[b] - spb[b] + sufb[b] + (sbrun + ahead[b]),
                        jnp.where(is_fox, s_fox[b] - m_fox, zb[b] - zmax[b]))
        p = jnp.exp(arg)
        psum.append(jnp.sum(p, axis=1, keepdims=True))
        pb = p.astype(BF16)
        pv.append(jnp.where(own_cols, _dot_nt(pb[:, :page], v_refs[2 * b][0, 0].astype(BF16)) +
                            _dot_nt(pb[:, page:], v_refs[2 * b + 1][0, 0].astype(BF16)), 0.0))

    m_new = jnp.where(is_fox, m_fox, m_old)
    alpha = jnp.exp(m_old - m_new)
    acc_ref[...] = alpha * acc_ref[...] + functools.reduce(lambda a, c: a + c, pv)
    l_ref[...] = jnp.broadcast_to(alpha * l_old + functools.reduce(lambda a, c: a + c, psum), l_ref.shape)
    m_ref[...] = jnp.broadcast_to(m_new, m_ref.shape)
    sbrun_ref[...] = jnp.broadcast_to(sbrun + jnp.where(is_sb, total, 0.0), sbrun_ref.shape)
    lrun_ref[...] = jnp.broadcast_to(lrun + jnp.where(is_fox, total, 0.0), lrun_ref.shape)

    def insert(ref, new, width, beats):
        old = [ref[k] for k in range(MOBA_TOPK)]
        new = jnp.broadcast_to(new, (nh, width))
        ref[2] = jnp.where(beats[1], old[1], jnp.where(beats[2], new, old[2]))
        ref[1] = jnp.where(beats[0], old[0], jnp.where(beats[1], new, old[1]))
        ref[0] = jnp.where(beats[0], new, old[0])

    for b in reversed(range(nb)):
        gate = gates[:, b:b + 1]
        beats = [gate >= sg_ref[k][:, 0:1] for k in range(MOBA_TOPK)]
        insert(sg_ref, gate, LANES, beats)
        insert(sm_ref, zmax[b], LANES, beats)
        insert(sl_ref, psum[b], LANES, beats)
        insert(sacc_ref, pv[b], nh * d, beats)

    @pl.when(n == n_steps - 1)
    def _():
        def own(x):
            parts = _dot(jnp.concatenate(_split3(x), axis=0), fold_ref[...])
            return parts[0:nh] + (parts[nh:2 * nh] + parts[2 * nh:3 * nh])

        qf = q.astype(F32)
        kn = kn_ref[0].astype(BF16).astype(F32)
        vn = vn_ref[0].astype(BF16).astype(F32)
        z_new = jnp.sum(qf * kn, axis=1, keepdims=True) * ATTN_SCALE
        acc = own(acc_ref[...])
        mf = jnp.maximum(m_ref[:, 0:1], z_new)
        af = jnp.exp(m_ref[:, 0:1] - mf)
        pn = jnp.exp(z_new - mf)
        fox_o = (af * acc + pn.astype(BF16).astype(F32) * vn) / (af * l_ref[:, 0:1] + pn)
        mm = z_new
        for k in range(MOBA_TOPK):
            mm = jnp.maximum(mm, jnp.where(sg_ref[k][:, 0:1] > -jnp.inf, sm_ref[k][:, 0:1], NEG))
        pn = jnp.exp(z_new - mm)
        num = pn.astype(BF16).astype(F32) * vn
        den = pn
        for k in range(MOBA_TOPK):
            wk = jnp.where(sg_ref[k][:, 0:1] > -jnp.inf, jnp.exp(sm_ref[k][:, 0:1] - mm), 0.0)
            num = num + wk * own(sacc_ref[k])
            den = den + wk * sl_ref[k][:, 0:1]
        moba_o = num / den
        o_ref[0] = jnp.where(is_sb, acc, jnp.where(is_fox, fox_o, moba_o))


def _decode_attention(layer, q, k_new, v_new, logf_new, ck, cv, clf, page_table):
    b = q.shape[0]
    n_pages = page_table.shape[1]
    page = ck.shape[3]
    pages = DECODE_PAGES
    assert n_pages % pages == 0 and pages % 2 == 0 and pages // 2 <= LANES and clf.shape[2] % 8 == 0
    n_steps = n_pages // pages
    nh, d = N_HEADS, HEAD_DIM
    eye = jnp.eye(nh, dtype=q.dtype)
    q_bd = (q[:, :, None, :] * eye[None, :, :, None]).reshape(b, nh, nh * d)
    fold = jnp.tile(jnp.eye(d, dtype=BF16), (nh, 1))

    def tok(shape):
        return pl.BlockSpec((1,) + shape, lambda bi, n, pt: (bi, 0, 0))

    def page_of(bi, n, pt, which):
        return pt[bi, pages * (n_steps - 1 - n) + which]

    def paged(which):
        return pl.BlockSpec((1, 1, nh * d, page), lambda bi, n, pt: (layer, page_of(bi, n, pt, which), 0, 0))

    def paged_logf(which):
        return pl.BlockSpec((1, H_FOX, 8, page), lambda bi, n, pt: (layer, 0, page_of(bi, n, pt, which) // 8, 0))

    slots = list(range(pages))
    grid_spec = pltpu.PrefetchScalarGridSpec(
        num_scalar_prefetch=1,
        grid=(b, n_steps),
        in_specs=[tok((nh, d)), tok((nh, nh * d)), tok((nh, d)), tok((nh, d)), tok((nh, 1)),
                  pl.BlockSpec((nh * d, d), lambda bi, n, pt: (0, 0))] +
                 [paged(s) for s in slots] + [paged(s) for s in slots] + [paged_logf(s) for s in slots],
        out_specs=tok((nh, d)),
        scratch_shapes=[pltpu.VMEM((nh, nh * d), F32)] + [pltpu.VMEM((nh, LANES), F32)] * 4 +
                       [pltpu.VMEM((MOBA_TOPK, nh, LANES), F32)] * 3 + [pltpu.VMEM((MOBA_TOPK, nh, nh * d), F32)],
    )
    return pl.pallas_call(
        functools.partial(_decode_kernel, page=page, n_steps=n_steps, pages=pages),
        grid_spec=grid_spec,
        out_shape=jax.ShapeDtypeStruct((b, nh, d), F32),
        compiler_params=_cparams("parallel", "arbitrary"),
        name="decode_attention",
    )(page_table, q, q_bd, k_new, v_new, logf_new, fold, *([ck] * pages + [cv] * pages + [clf] * pages))


def _layer_norm(x, g, b):
    mu = jnp.mean(x, axis=-1, keepdims=True)
    var = jnp.mean(jnp.square(x - mu), axis=-1, keepdims=True)
    return (x - mu) * lax.rsqrt(var + LN_EPS) * g + b


def _rope_tables(pos):
    half = ROPE_DIM // 2
    inv = ROPE_THETA ** (-jnp.arange(half, dtype=F32) / half)
    ang = pos.astype(F32)[:, None] * inv
    cos, sin = jnp.cos(ang), jnp.sin(ang)
    t = pos.shape[0]
    one = jnp.ones((t, HEAD_DIM - ROPE_DIM), F32)
    zero = jnp.zeros((t, HEAD_DIM - ROPE_DIM), F32)
    zh = jnp.zeros((t, half), F32)
    c = jnp.concatenate([cos, cos, one], axis=1)
    s_hi = jnp.concatenate([zh, sin, zero], axis=1)
    s_lo = jnp.concatenate([-sin, zh, zero], axis=1)
    return tuple(jnp.tile(a, (1, H_MOBA)) for a in (c, s_hi, s_lo))


def _rope_flat(x, tables):
    c, s_hi, s_lo = tables
    half = ROPE_DIM // 2
    return x * c + jnp.roll(x, half, axis=-1) * s_hi + jnp.roll(x, -half, axis=-1) * s_lo


def _project(x2d, w_in_p, b_f, tm):
    hd = N_HEADS * HEAD_DIM
    p = _mm(x2d, w_in_p, tm=tm, tn=640)
    logf = jax.nn.log_sigmoid(p[:, 3 * hd:3 * hd + H_FOX] + b_f)
    return p[:, :hd], p[:, hd:2 * hd], p[:, 2 * hd:3 * hd], logf


def _merge(o2d, norm_g, w_out_b, tm):
    a, bnd = H_SB * HEAD_DIM, (H_SB + H_FOX) * HEAD_DIM
    parts = []
    for lo, hi in ((0, a), (a, bnd), (bnd, D_MODEL)):
        of = o2d[:, lo:hi]
        parts.append(of * lax.rsqrt(jnp.mean(jnp.square(of), axis=-1, keepdims=True) + RMS_EPS))
    h = jnp.concatenate(parts, axis=-1) * norm_g
    return _mm(h, w_out_b, tm=tm, tn=D_MODEL)


def _heads(x2d, b, t):
    return jnp.transpose(x2d.reshape(b, t, N_HEADS, HEAD_DIM), (0, 2, 1, 3))


def _mix_prompt(x2d, b, t, w_in_p, b_f, norm_g, w_out_b):
    q, k, v, logf = _project(x2d, w_in_p, b_f, tm=1024)
    mo = (H_SB + H_FOX) * HEAD_DIM
    nb = t // MOBA_BLOCK
    assert nb <= MOBA_LANES
    tables = _rope_tables(jnp.arange(t))
    q = jnp.concatenate([q[:, :mo], _rope_flat(q[:, mo:].reshape(b, t, -1), tables).reshape(b * t, -1)], axis=1)
    k = jnp.concatenate([k[:, :mo], _rope_flat(k[:, mo:].reshape(b, t, -1), tables).reshape(b * t, -1)], axis=1)
    qh = _heads(q * ATTN_SCALE, b, t).astype(BF16)
    kh = _heads(k, b, t).astype(BF16)
    vh = _heads(v, b, t).astype(BF16)
    s0, s1 = H_SB, H_SB + H_FOX
    sb_tk = ATT_TILES["sb"][1]
    tri = jnp.arange(sb_tk)[:, None] >= jnp.arange(sb_tk)[None, :]
    u2 = jnp.concatenate([tri, tri], axis=0).astype(BF16)
    oa = _prompt_attention("sb", qh[:, :s0], kh[:, :s0], vh[:, :s0], u2)
    neg_c = -jnp.transpose(jnp.cumsum(logf.reshape(b, t, H_FOX), axis=1), (0, 2, 1))
    fill = jnp.zeros((b, H_FOX, t, LANES - HEAD_DIM - 3), BF16)
    k_fox = jnp.concatenate([kh[:, s0:s1]] + [p[..., None] for p in _split3(neg_c)] + [fill], axis=-1)
    q_fox = jnp.concatenate([qh[:, s0:s1], jnp.ones((b, H_FOX, t, 3), BF16), fill], axis=-1)
    ob = _prompt_attention("fox", q_fox, k_fox, vh[:, s0:s1], None)
    onehot = (jnp.arange(t)[:, None] // MOBA_BLOCK == jnp.arange(MOBA_LANES)[None, :]).astype(BF16)
    k_mo = jnp.concatenate([kh[:, s1:], jnp.broadcast_to(onehot, (b, H_MOBA, t, MOBA_LANES)),
                            jnp.zeros((b, H_MOBA, t, LANES - HEAD_DIM - MOBA_LANES), BF16)], axis=-1)
    q_mo = jnp.pad(qh[:, s1:], ((0, 0), (0, 0), (0, 0), (0, LANES - HEAD_DIM)))
    k_mean = jnp.mean(k[:, mo:].reshape(b, nb, MOBA_BLOCK, H_MOBA, HEAD_DIM), axis=2)
    k_mean = jnp.transpose(k_mean, (0, 2, 1, 3)).astype(BF16)
    km = jnp.pad(k_mean, ((0, 0), (0, 0), (HEAD_DIM, LANES - HEAD_DIM - nb), (0, LANES - HEAD_DIM)))
    oc = _prompt_attention("moba", q_mo, k_mo, vh[:, s1:], km)
    o = jnp.transpose(jnp.concatenate([oa, ob, oc], axis=1), (0, 2, 1, 3)).reshape(b * t, D_MODEL)
    y = _merge(o, norm_g, w_out_b, tm=1024)
    return y, k, v, logf


def _mix_sample(layer, x2d, ck, cv, clf, page_table, past, w_in_p, b_f, norm_g, w_out_b):
    b = x2d.shape[0]
    q, k, v, logf = _project(x2d, w_in_p, b_f, tm=b)
    mo = (H_SB + H_FOX) * HEAD_DIM
    tables = _rope_tables(past + jnp.arange(1))
    q = jnp.concatenate([q[:, :mo], _rope_flat(q[:, mo:].reshape(b, 1, -1), tables).reshape(b, -1)], axis=1)
    k = jnp.concatenate([k[:, :mo], _rope_flat(k[:, mo:].reshape(b, 1, -1), tables).reshape(b, -1)], axis=1)
    lf16 = jnp.pad(logf, ((0, 0), (H_SB, H_MOBA)))[:, :, None]
    o = _decode_attention(layer, q.reshape(b, N_HEADS, HEAD_DIM).astype(BF16), k.reshape(b, N_HEADS, HEAD_DIM),
                          v.reshape(b, N_HEADS, HEAD_DIM), lf16, ck, cv, clf, page_table)
    y = _merge(o.reshape(b, D_MODEL), norm_g, w_out_b, tm=b)
    return y, k, v, logf


def _dense_ffn(x2d, w1, w3, w2, tm):
    rows = x2d.shape[0]
    return _ffn(x2d, jnp.zeros((rows // tm,), jnp.int32), w1, w3, w2, tm)


def _moe_ffn(x2d, w_r_p, b_r, w1, w3, w2, tm):
    n, _ = x2d.shape
    logits = _mm(x2d, w_r_p, tm=min(n, 1024), tn=LANES, precise=True)[:, :N_EXPERTS] + b_r
    top_v, top_i = lax.top_k(logits, TOP_K)
    gates = jax.nn.softmax(top_v, axis=-1)
    a = n * TOP_K
    e = top_i.reshape(a)
    onehot = (e[:, None] == jnp.arange(N_EXPERTS)[None, :]).astype(jnp.int32)
    before = jnp.cumsum(onehot, axis=0) - onehot
    counts = jnp.sum(onehot, axis=0)
    padded = (counts + tm - 1) // tm * tm
    pad_end = jnp.cumsum(padded)
    dest = jnp.sum(onehot * (before + (pad_end - padded)[None, :]), axis=1).astype(jnp.int32)
    n_blocks = (a + N_EXPERTS * (tm - 1) + tm - 1) // tm
    rows = n_blocks * tm
    tok_buf = jnp.zeros((rows,), jnp.int32).at[dest].set(jnp.arange(a, dtype=jnp.int32) // TOP_K)
    blk_expert = jnp.minimum(jnp.sum(pad_end[None, :] <= (jnp.arange(n_blocks) * tm)[:, None], axis=1),
                             N_EXPERTS - 1).astype(jnp.int32)
    y_buf = _ffn(x2d[tok_buf], blk_expert, w1, w3, w2, tm)
    where = dest.reshape(n, TOP_K)
    return y_buf[where[:, 0]] * gates[:, 0:1] + y_buf[where[:, 1]] * gates[:, 1:2]


def kernel(x_prompt, x_sample, cache_k, cache_v, cache_logf, page_table, w_in, b_f, mix_norm_g, w_out,
           ln1_g, ln1_b, ln2_g, ln2_b, ffn_w1, ffn_w3, ffn_w2, router_w, router_b, moe_w1, moe_w3, moe_w2):
    b, t, d = x_prompt.shape
    bs = x_sample.shape[0]
    depth, pool, page = cache_k.shape[0], cache_k.shape[1], cache_k.shape[2]
    past = page_table.shape[1] * page
    hp = x_prompt.reshape(b * t, d)
    hs = x_sample.reshape(bs, d)
    ck = jnp.transpose(cache_k, (0, 1, 3, 4, 2)).reshape(depth, pool, N_HEADS * HEAD_DIM, page)
    cv = jnp.transpose(cache_v, (0, 1, 3, 4, 2)).reshape(depth, pool, N_HEADS * HEAD_DIM, page)
    clf = jnp.transpose(cache_logf, (0, 3, 1, 2))
    col_pad = (-w_in.shape[2]) % 640
    outs = [[] for _ in range(6)]
    for l in range(depth):
        w_in_p = jnp.pad(w_in[l], ((0, 0), (0, col_pad))).astype(BF16)
        w_out_b = w_out[l].astype(BF16)
        mp, k1, v1, f1 = _mix_prompt(hp, b, t, w_in_p, b_f[l], mix_norm_g[l], w_out_b)
        ms, k2, v2, f2 = _mix_sample(l, hs, ck, cv, clf, page_table, past, w_in_p, b_f[l], mix_norm_g[l], w_out_b)
        hp = _layer_norm(DEEPNORM_ALPHA * hp + mp, ln1_g[l], ln1_b[l])
        hs = _layer_norm(DEEPNORM_ALPHA * hs + ms, ln1_g[l], ln1_b[l])
        j = l // 2
        if l % 2 == 0:
            w1, w3, w2 = (w[j:j + 1].astype(BF16) for w in (ffn_w1, ffn_w3, ffn_w2))
            fp = _dense_ffn(hp, w1, w3, w2, tm=512)
            fs = _dense_ffn(hs, w1, w3, w2, tm=bs)
        else:
            w1, w3, w2 = (w[j].astype(BF16) for w in (moe_w1, moe_w3, moe_w2))
            w_r_p = jnp.pad(router_w[j], ((0, 0), (0, LANES - N_EXPERTS)))
            fp = _moe_ffn(hp, w_r_p, router_b[j], w1, w3, w2, tm=256)
            fs = _moe_ffn(hs, w_r_p, router_b[j], w1, w3, w2, tm=32)
        hp = _layer_norm(DEEPNORM_ALPHA * hp + fp, ln2_g[l], ln2_b[l])
        hs = _layer_norm(DEEPNORM_ALPHA * hs + fs, ln2_g[l], ln2_b[l])
        for lst, val in zip(outs, (k1.reshape(b, t, N_HEADS, HEAD_DIM), v1.reshape(b, t, N_HEADS, HEAD_DIM),
                                   f1.reshape(b, t, H_FOX), k2.reshape(bs, 1, N_HEADS, HEAD_DIM),
                                   v2.reshape(bs, 1, N_HEADS, HEAD_DIM), f2.reshape(bs, 1, H_FOX))):
            lst.append(val)
    return (hp.reshape(b, t, d), hs.reshape(bs, 1, d)) + tuple(jnp.stack(o) for o in outs)
```
